```python
import math
import jax, jax.numpy as jnp
from jax import lax
import numpy as np

D_MODEL = 2048
BATCH = 1
SEQ = 16384
DEPTH = 2

CONV_WIDTH = D_MODEL // 2
CONV_K = 3
RET_HEADS = 8
RET_QK_DIM = D_MODEL // (2 * RET_HEADS)
RET_V_DIM = 2 * RET_QK_DIM
RET_KEY_WIDTH = RET_HEADS * RET_QK_DIM
RET_VALUE_WIDTH = RET_HEADS * RET_V_DIM
CHUNK = 128
ROPE_BASE = 10000.0
D_FF = ((8 * D_MODEL // 3 + 255) // 256) * 256
IN_WIDTH = 3 * CONV_WIDTH + 2 * RET_KEY_WIDTH + 2 * RET_VALUE_WIDTH + 2 * D_MODEL
EPS = 1e-6

kernel_name = "hybrid_shortconv_retention_gated_block"


def rms_norm(x, g):
    xf = x.astype(jnp.float32)
    y = xf * lax.rsqrt(jnp.mean(xf * xf, axis=-1, keepdims=True) + EPS)
    return (y * g.astype(jnp.float32)).astype(x.dtype)


def rotary(x):
    s, d = x.shape[1], x.shape[-1]
    half = d // 2
    inv = ROPE_BASE ** (-jnp.arange(half, dtype=jnp.float32) / half)
    ang = jnp.arange(s, dtype=jnp.float32)[:, None] * inv[None, :]
    cos = jnp.cos(ang)[None, :, None, :]
    sin = jnp.sin(ang)[None, :, None, :]
    x1, x2 = x[..., :half], x[..., half:]
    return jnp.concatenate([x1 * cos - x2 * sin, x2 * cos + x1 * sin], axis=-1).astype(x.dtype)


def retention(q, k, v):
    b, s, h, dk = q.shape
    dv = v.shape[-1]
    n = s // CHUNK
    to_chunks = lambda t: t.reshape(b, n, CHUNK, h, t.shape[-1]).transpose(0, 3, 1, 2, 4)
    qc, kc, vc = to_chunks(q), to_chunks(k), to_chunks(v)
    log_gamma = jnp.log(1.0 - 2.0 ** (-5.0 - jnp.arange(h, dtype=jnp.float32)))
    idx = jnp.arange(CHUNK, dtype=jnp.float32)
    diff = idx[:, None] - idx[None, :]
    decay_mask = jnp.where(diff[None] >= 0,
                           jnp.exp(log_gamma[:, None, None] * jnp.maximum(diff, 0.0)[None]),
                           0.0)
    scores = jnp.einsum('bhncd,bhnmd->bhncm', qc, kc) * decay_mask[None, :, None]
    intra = jnp.einsum('bhncm,bhnme->bhnce', scores, vc)
    q_decay = jnp.exp(log_gamma[:, None] * (idx + 1.0))
    k_decay = jnp.exp(log_gamma[:, None] * (CHUNK - 1.0 - idx))
    chunk_decay = jnp.exp(log_gamma * CHUNK)[None, :, None, None]
    kv = jnp.einsum('bhnmd,bhnme->bhnde', kc * k_decay[None, :, None, :, None], vc)
    qd = qc * q_decay[None, :, None, :, None]
    kv_t = kv.transpose(2, 0, 1, 3, 4)
    qd_t = qd.transpose(2, 0, 1, 3, 4)

    def step(state, xs):
        kv_n, q_n = xs
        out = jnp.einsum('bhcd,bhde->bhce', q_n, state)
        state = chunk_decay.astype(state.dtype) * state + kv_n
        return state, out

    init = jnp.zeros((b, h, dk, dv), dtype=kv.dtype)
    _, inter = lax.scan(step, init, (kv_t, qd_t))
    inter = inter.transpose(1, 2, 0, 3, 4)
    o = (intra + inter).transpose(0, 2, 3, 1, 4).reshape(b, s, h, dv)
    return o.astype(v.dtype)


def head_rms_norm(o):
    of = o.astype(jnp.float32)
    return (of * lax.rsqrt(jnp.mean(of * of, axis=-1, keepdims=True) + EPS)).astype(o.dtype)


def setup_inputs(seed: int = 0) -> dict:
    key = jax.random.key(seed)
    ks = jax.random.split(key, 16)
    nrm = lambda k, shape, fan_in: jax.random.normal(k, shape, jnp.float32) * (fan_in ** -0.5)
    x = jax.random.normal(ks[0], (BATCH, SEQ, D_MODEL), jnp.float32)
    g_mix = 1.0 + 0.02 * jax.random.normal(ks[1], (DEPTH, D_MODEL), jnp.float32)
    w_in = nrm(ks[2], (DEPTH, D_MODEL, IN_WIDTH), D_MODEL)
    w_conv = nrm(ks[3], (DEPTH, CONV_K, 1, CONV_WIDTH), CONV_K)
    w_conv_out = nrm(ks[4], (DEPTH, CONV_WIDTH, D_MODEL), CONV_WIDTH)
    w_ret_out = nrm(ks[5], (DEPTH, RET_VALUE_WIDTH, D_MODEL), RET_VALUE_WIDTH)
    w_out = nrm(ks[6], (DEPTH, D_MODEL, D_MODEL), D_MODEL)
    g_ffn = 1.0 + 0.02 * jax.random.normal(ks[7], (DEPTH, D_MODEL), jnp.float32)
    w_ffn_in = nrm(ks[8], (DEPTH, D_MODEL, 2 * D_FF), D_MODEL)
    w_ffn_out = nrm(ks[9], (DEPTH, D_FF, D_MODEL), D_FF)
    g_final = 1.0 + 0.02 * jax.random.normal(ks[10], (D_MODEL,), jnp.float32)
    return {"x": x, "g_mix": g_mix, "w_in": w_in, "w_conv": w_conv,
            "w_conv_out": w_conv_out, "w_ret_out": w_ret_out, "w_out": w_out,
            "g_ffn": g_ffn, "w_ffn_in": w_ffn_in, "w_ffn_out": w_ffn_out,
            "g_final": g_final}


def reference(x, g_mix, w_in, w_conv, w_conv_out, w_ret_out, w_out,
              g_ffn, w_ffn_in, w_ffn_out, g_final):
    bsz, seq, _ = x.shape
    cuts = np.cumsum([CONV_WIDTH, CONV_WIDTH, CONV_WIDTH,
                      RET_KEY_WIDTH, RET_KEY_WIDTH, RET_VALUE_WIDTH, RET_VALUE_WIDTH,
                      D_MODEL]).tolist()
    for l in range(DEPTH):
        h = rms_norm(x, g_mix[l])
        proj = h @ w_in[l]
        b_g, c_g, u, q, k, v, g, a_conv, a_ret = jnp.split(proj, cuts, axis=-1)
        uc = c_g * u
        conv = lax.conv_general_dilated(uc, w_conv[l], window_strides=(1,),
                                        padding=[(CONV_K - 1, 0)],
                                        dimension_numbers=('NWC', 'WIO', 'NWC'),
                                        feature_group_count=CONV_WIDTH)
        y_conv = (b_g * conv) @ w_conv_out[l]
        q = rotary(q.reshape(bsz, seq, RET_HEADS, RET_QK_DIM))
        k = rotary(k.reshape(bsz, seq, RET_HEADS, RET_QK_DIM)) * (RET_QK_DIM ** -0.5)
        v = v.reshape(bsz, seq, RET_HEADS, RET_V_DIM)
        o = head_rms_norm(retention(q, k, v)).reshape(bsz, seq, RET_VALUE_WIDTH)
        y_ret = (jax.nn.silu(g) * o) @ w_ret_out[l]
        merged = jax.nn.sigmoid(a_conv) * y_conv + jax.nn.sigmoid(a_ret) * y_ret
        x = x + merged @ w_out[l]
        h2 = rms_norm(x, g_ffn[l])
        gate, up = jnp.split(h2 @ w_ffn_in[l], 2, axis=-1)
        x = x + (jax.nn.silu(gate) * up) @ w_ffn_out[l]
    return rms_norm(x, g_final)
```

```python
import functools

import jax
import jax.numpy as jnp
from jax.experimental import pallas as pl
from jax.experimental.pallas import tpu as pltpu

D_MODEL = 2048
CONV_WIDTH = D_MODEL // 2
CONV_K = 3
RET_HEADS = 8
RET_QK_DIM = D_MODEL // (2 * RET_HEADS)
RET_V_DIM = 2 * RET_QK_DIM
RET_KEY_WIDTH = RET_HEADS * RET_QK_DIM
RET_VALUE_WIDTH = RET_HEADS * RET_V_DIM
CHUNK = 128
ROPE_BASE = 10000.0
D_FF = ((8 * D_MODEL // 3 + 255) // 256) * 256
IN_WIDTH = 3 * CONV_WIDTH + 2 * RET_KEY_WIDTH + 2 * RET_VALUE_WIDTH + 2 * D_MODEL
EPS = 1e-6

COL_BLOCK = 1024
COL_B, COL_C, COL_U, COL_Q, COL_K = 0, 1, 2, 3, 4
COL_V0, COL_V1, COL_G0, COL_G1 = 5, 6, 7, 8
COL_ACONV = 9
COL_ARET = 11

V7X_VMEM_LIMIT_BYTES = 56 * 1024 * 1024
SUBLANES = 8

NORM_TM = 512
INPROJ_TM = 1024
INPROJ_TN = 1024
MIX_ROWS = CHUNK
OUT_TM = 256
FFN_TM = 512
FFN_TF = 512


def _compiler_params(semantics):
    return pltpu.CompilerParams(dimension_semantics=semantics,
                                vmem_limit_bytes=V7X_VMEM_LIMIT_BYTES)


def _rms_norm_f32(x, g):
    var = jnp.mean(x * x, axis=-1, keepdims=True)
    return x * jax.lax.rsqrt(var + EPS) * g


def _sigmoid(x):
    return 1.0 / (1.0 + jnp.exp(-x))


def _norm_kernel(x_ref, g_ref, o_ref):
    o_ref[...] = _rms_norm_f32(x_ref[...], g_ref[...]).astype(o_ref.dtype)


def _rms_norm(x, g, out_dtype):
    s, d = x.shape
    return pl.pallas_call(
        _norm_kernel,
        grid=(s // NORM_TM,),
        in_specs=[pl.BlockSpec((NORM_TM, d), lambda i: (i, 0)),
                  pl.BlockSpec((1, d), lambda i: (0, 0))],
        out_specs=pl.BlockSpec((NORM_TM, d), lambda i: (i, 0)),
        out_shape=jax.ShapeDtypeStruct((s, d), out_dtype),
        compiler_params=_compiler_params(("arbitrary",)),
        name="rms_norm",
    )(x, g.reshape(1, d))


def _matmul_kernel(x_ref, w_ref, o_ref):
    o_ref[...] = jnp.dot(x_ref[...], w_ref[...],
                         preferred_element_type=jnp.float32).astype(o_ref.dtype)


def _in_proj(h, w):
    s, d = h.shape
    n = w.shape[1]
    return pl.pallas_call(
        _matmul_kernel,
        grid=(n // INPROJ_TN, s // INPROJ_TM),
        in_specs=[pl.BlockSpec((INPROJ_TM, d), lambda j, i: (i, 0)),
                  pl.BlockSpec((d, INPROJ_TN), lambda j, i: (0, j))],
        out_specs=pl.BlockSpec((INPROJ_TM, INPROJ_TN), lambda j, i: (i, j)),
        out_shape=jax.ShapeDtypeStruct((s, n), jnp.bfloat16),
        compiler_params=_compiler_params(("arbitrary", "arbitrary")),
        name="in_proj",
    )(h, w)


def _mixer_kernel(b_ref, c_ref, u_ref, q_ref, k_ref, v0_ref, v1_ref, g0_ref, g1_ref,
                  wconv_ref, qcos_ref, qsin_ref, kcos_ref, ksin_ref,
                  mask_ref, qdec_ref, kdec_ref, cdec_ref,
                  bc_ref, gr_ref, ucbuf_ref, state_ref):
    step = pl.program_id(0)
    rows = MIX_ROWS

    @pl.when(step == 0)
    def _():
        ucbuf_ref[0:SUBLANES, :] = jnp.zeros((SUBLANES, CONV_WIDTH), jnp.float32)
        state_ref[...] = jnp.zeros_like(state_ref)

    uc = c_ref[...].astype(jnp.float32) * u_ref[...].astype(jnp.float32)
    ucbuf_ref[SUBLANES:SUBLANES + rows, :] = uc
    prev1 = ucbuf_ref[SUBLANES - 1:SUBLANES - 1 + rows, :]
    prev2 = ucbuf_ref[SUBLANES - 2:SUBLANES - 2 + rows, :]
    conv = (wconv_ref[0:1, :] * prev2 + wconv_ref[1:2, :] * prev1
            + wconv_ref[2:3, :] * uc)
    bc_ref[...] = (b_ref[...].astype(jnp.float32) * conv).astype(bc_ref.dtype)
    ucbuf_ref[0:SUBLANES, :] = uc[rows - SUBLANES:rows, :]

    qcos, qsin = qcos_ref[...], qsin_ref[...]
    kcos, ksin = kcos_ref[...], ksin_ref[...]
    half = RET_QK_DIM // 2
    for h in range(RET_HEADS):
        qk_cols = slice(h * RET_QK_DIM, (h + 1) * RET_QK_DIM)
        q = q_ref[:, qk_cols].astype(jnp.float32)
        k = k_ref[:, qk_cols].astype(jnp.float32)
        q = q * qcos + pltpu.roll(q, half, axis=1) * qsin
        k = k * kcos + pltpu.roll(k, half, axis=1) * ksin
        v_ref, g_ref = (v0_ref, g0_ref) if h < RET_HEADS // 2 else (v1_ref, g1_ref)
        hv = h % (RET_HEADS // 2)
        v_cols = slice(hv * RET_V_DIM, (hv + 1) * RET_V_DIM)
        v = v_ref[:, v_cols]
        qb = q.astype(jnp.bfloat16)
        kb = k.astype(jnp.bfloat16)
        scores = jax.lax.dot_general(qb, kb, (((1,), (1,)), ((), ())),
                                     preferred_element_type=jnp.float32)
        scores = scores * mask_ref[h]
        intra = jnp.dot(scores.astype(jnp.bfloat16), v,
                        preferred_element_type=jnp.float32)
        state = state_ref[h]
        inter = jnp.dot((q * qdec_ref[h]).astype(jnp.bfloat16),
                        state.astype(jnp.bfloat16),
                        preferred_element_type=jnp.float32)
        kd = (k * kdec_ref[h]).astype(jnp.bfloat16)
        kv = jax.lax.dot_general(kd, v, (((0,), (0,)), ((), ())),
                                 preferred_element_type=jnp.float32)
        state_ref[h] = cdec_ref[h] * state + kv
        o = intra + inter
        o = o * jax.lax.rsqrt(jnp.mean(o * o, axis=-1, keepdims=True) + EPS)
        g = g_ref[:, v_cols].astype(jnp.float32)
        out_cols = slice(h * RET_V_DIM, (h + 1) * RET_V_DIM)
        gr_ref[:, out_cols] = (g * _sigmoid(g) * o).astype(gr_ref.dtype)


def _retention_tables(seq):
    half = RET_QK_DIM // 2
    inv = ROPE_BASE ** (-jnp.arange(half, dtype=jnp.float32) / half)
    ang = jnp.arange(seq, dtype=jnp.float32)[:, None] * inv[None, :]
    cos, sin = jnp.cos(ang), jnp.sin(ang)
    cos2 = jnp.concatenate([cos, cos], axis=-1)
    sin2 = jnp.concatenate([-sin, sin], axis=-1)
    kscale = RET_QK_DIM ** -0.5
    log_gamma = jnp.log(1.0 - 2.0 ** (-5.0 - jnp.arange(RET_HEADS, dtype=jnp.float32)))
    idx = jnp.arange(CHUNK, dtype=jnp.float32)
    diff = idx[:, None] - idx[None, :]
    mask = jnp.where(diff[None] >= 0,
                     jnp.exp(log_gamma[:, None, None] * jnp.maximum(diff, 0.0)[None]), 0.0)
    q_decay = jnp.exp(log_gamma[:, None] * (idx + 1.0))
    k_decay = jnp.exp(log_gamma[:, None] * (CHUNK - 1.0 - idx))
    chunk_decay = jnp.exp(log_gamma * CHUNK)
    lanes = (RET_HEADS, CHUNK, RET_QK_DIM)
    return dict(
        qcos=cos2, qsin=sin2, kcos=cos2 * kscale, ksin=sin2 * kscale, mask=mask,
        qdec=jnp.broadcast_to(q_decay[:, :, None], lanes),
        kdec=jnp.broadcast_to(k_decay[:, :, None], lanes),
        cdec=jnp.broadcast_to(chunk_decay[:, None, None], (RET_HEADS, 1, RET_V_DIM)))


def _mixers(proj, w_conv, tables):
    s = proj.shape[0]
    rows = MIX_ROWS
    col = lambda c: pl.BlockSpec((rows, COL_BLOCK), lambda i, c=c: (i, c))
    rope = pl.BlockSpec((rows, RET_QK_DIM), lambda i: (i, 0))
    whole = lambda shape: pl.BlockSpec(shape, lambda i: (0,) * len(shape))
    return pl.pallas_call(
        _mixer_kernel,
        grid=(s // rows,),
        in_specs=[col(COL_B), col(COL_C), col(COL_U), col(COL_Q), col(COL_K),
                  col(COL_V0), col(COL_V1), col(COL_G0), col(COL_G1),
                  whole((CONV_K, CONV_WIDTH)),
                  rope, rope, rope, rope,
                  whole((RET_HEADS, CHUNK, CHUNK)),
                  whole((RET_HEADS, CHUNK, RET_QK_DIM)),
                  whole((RET_HEADS, CHUNK, RET_QK_DIM)),
                  whole((RET_HEADS, 1, RET_V_DIM))],
        out_specs=[pl.BlockSpec((rows, CONV_WIDTH), lambda i: (i, 0)),
                   pl.BlockSpec((rows, RET_VALUE_WIDTH), lambda i: (i, 0))],
        out_shape=[jax.ShapeDtypeStruct((s, CONV_WIDTH), jnp.bfloat16),
                   jax.ShapeDtypeStruct((s, RET_VALUE_WIDTH), jnp.bfloat16)],
        scratch_shapes=[pltpu.VMEM((SUBLANES + rows, CONV_WIDTH), jnp.float32),
                        pltpu.VMEM((RET_HEADS, RET_QK_DIM, RET_V_DIM), jnp.float32)],
        compiler_params=_compiler_params(("arbitrary",)),
        name="mixers",
    )(*([proj] * 9), w_conv,
      tables["qcos"], tables["qsin"], tables["kcos"], tables["ksin"],
      tables["mask"], tables["qdec"], tables["kdec"], tables["cdec"])


def _out_kernel(bc_ref, gr_ref, ac0_ref, ac1_ref, ar0_ref, ar1_ref, x_ref,
                wco_ref, wro_ref, wo_ref, g_ref, x1_ref, h2_ref):
    y_conv = jnp.dot(bc_ref[...], wco_ref[...], preferred_element_type=jnp.float32)
    y_ret = jnp.dot(gr_ref[...], wro_ref[...], preferred_element_type=jnp.float32)
    a_conv = jnp.concatenate([ac0_ref[...], ac1_ref[...]], axis=-1).astype(jnp.float32)
    a_ret = jnp.concatenate([ar0_ref[...], ar1_ref[...]], axis=-1).astype(jnp.float32)
    merged = _sigmoid(a_conv) * y_conv + _sigmoid(a_ret) * y_ret
    x1 = x_ref[...] + jnp.dot(merged.astype(jnp.bfloat16), wo_ref[...],
                              preferred_element_type=jnp.float32)
    x1_ref[...] = x1
    h2_ref[...] = _rms_norm_f32(x1, g_ref[...]).astype(h2_ref.dtype)


def _out_proj(bc, gr, proj, x, w_co, w_ro, w_o, g_ffn):
    s, d = x.shape
    tm = OUT_TM
    rows = lambda width: pl.BlockSpec((tm, width), lambda i: (i, 0))
    col = lambda c: pl.BlockSpec((tm, COL_BLOCK), lambda i, c=c: (i, c))
    whole = lambda shape: pl.BlockSpec(shape, lambda i: (0, 0),
                                       pipeline_mode=pl.Buffered(1))
    return pl.pallas_call(
        _out_kernel,
        grid=(s // tm,),
        in_specs=[rows(CONV_WIDTH), rows(RET_VALUE_WIDTH),
                  col(COL_ACONV), col(COL_ACONV + 1), col(COL_ARET), col(COL_ARET + 1),
                  rows(d),
                  whole(w_co.shape), whole(w_ro.shape), whole(w_o.shape),
                  whole((1, d))],
        out_specs=[rows(d), rows(d)],
        out_shape=[jax.ShapeDtypeStruct((s, d), jnp.float32),
                   jax.ShapeDtypeStruct((s, d), jnp.bfloat16)],
        compiler_params=_compiler_params(("arbitrary",)),
        name="out_proj",
    )(bc, gr, proj, proj, proj, proj, x, w_co, w_ro, w_o, g_ffn.reshape(1, d))


def _ffn_kernel(h2_ref, x1_ref, wg_ref, wu_ref, wout_ref, g_ref, x2_ref, hn_ref, acc_ref,
                *, final):
    f = pl.program_id(1)
    h2 = h2_ref[...]
    gate = jnp.dot(h2, wg_ref[...], preferred_element_type=jnp.float32)
    up = jnp.dot(h2, wu_ref[...], preferred_element_type=jnp.float32)
    act = (gate * _sigmoid(gate) * up).astype(jnp.bfloat16)
    part = jnp.dot(act, wout_ref[...], preferred_element_type=jnp.float32)

    @pl.when(f == 0)
    def _():
        acc_ref[...] = x1_ref[...] + part

    @pl.when(f > 0)
    def _():
        acc_ref[...] += part

    @pl.when(f == pl.num_programs(1) - 1)
    def _():
        x2 = acc_ref[...]
        normed = _rms_norm_f32(x2, g_ref[...])
        if final:
            x2_ref[...] = normed
        else:
            x2_ref[...] = x2
        hn_ref[...] = normed.astype(hn_ref.dtype)


def _ffn(h2, x1, w_in, w_out, g_next, final):
    s, d = x1.shape
    tm, tf = FFN_TM, FFN_TF
    nf = D_FF // tf
    return pl.pallas_call(
        functools.partial(_ffn_kernel, final=final),
        grid=(s // tm, nf),
        in_specs=[pl.BlockSpec((tm, d), lambda i, f: (i, 0)),
                  pl.BlockSpec((tm, d), lambda i, f: (i, 0)),
                  pl.BlockSpec((d, tf), lambda i, f: (0, f)),
                  pl.BlockSpec((d, tf), lambda i, f, nf=nf: (0, f + nf)),
                  pl.BlockSpec((tf, d), lambda i, f: (f, 0)),
                  pl.BlockSpec((1, d), lambda i, f: (0, 0))],
        out_specs=[pl.BlockSpec((tm, d), lambda i, f: (i, 0)),
                   pl.BlockSpec((tm, d), lambda i, f: (i, 0))],
        out_shape=[jax.ShapeDtypeStruct((s, d), jnp.float32),
                   jax.ShapeDtypeStruct((s, d), jnp.bfloat16)],
        scratch_shapes=[pltpu.VMEM((tm, d), jnp.float32)],
        compiler_params=_compiler_params(("arbitrary", "arbitrary")),
        name="ffn",
    )(h2, x1, w_in, w_in, w_out, g_next.reshape(1, d))


def kernel(x, g_mix, w_in, w_conv, w_conv_out, w_ret_out, w_out, g_ffn, w_ffn_in,
           w_ffn_out, g_final):
    bsz, seq, d = x.shape
    depth = w_in.shape[0]
    assert bsz == 1 and d == D_MODEL and seq % INPROJ_TM == 0
    bf16 = jnp.bfloat16
    tables = _retention_tables(seq)
    xs = x.reshape(seq, d)
    h = _rms_norm(xs, g_mix[0], bf16)
    for l in range(depth):
        proj = _in_proj(h, w_in[l].astype(bf16))
        bc, gr = _mixers(proj, w_conv[l].reshape(CONV_K, CONV_WIDTH), tables)
        x1, h2 = _out_proj(bc, gr, proj, xs, w_conv_out[l].astype(bf16),
                           w_ret_out[l].astype(bf16), w_out[l].astype(bf16), g_ffn[l])
        final = l == depth - 1
        g_next = g_final if final else g_mix[l + 1]
        xs, h = _ffn(h2, x1, w_ffn_in[l].astype(bf16), w_ffn_out[l].astype(bf16),
                     g_next, final)
    return xs.reshape(bsz, seq, d)
```

```python
import functools

import jax
import jax.numpy as jnp
from jax.experimental import pallas as pl
from jax.experimental.pallas import tpu as pltpu

D_MODEL = 2048
CONV_WIDTH = D_MODEL // 2
CONV_K = 3
RET_HEADS = 8
RET_QK_DIM = D_MODEL // (2 * RET_HEADS)
RET_V_DIM = 2 * RET_QK_DIM
RET_KEY_WIDTH = RET_HEADS * RET_QK_DIM
RET_VALUE_WIDTH = RET_HEADS * RET_V_DIM
CHUNK = 128
ROPE_BASE = 10000.0
D_FF = ((8 * D_MODEL // 3 + 255) // 256) * 256
IN_WIDTH = 3 * CONV_WIDTH + 2 * RET_KEY_WIDTH + 2 * RET_VALUE_WIDTH + 2 * D_MODEL
EPS = 1e-6

COL_BLOCK = 1024
COL_B, COL_C, COL_U, COL_Q, COL_K = 0, 1, 2, 3, 4
COL_V0, COL_V1, COL_G0, COL_G1 = 5, 6, 7, 8
COL_ACONV = 9
COL_ARET = 11

V7X_VMEM_LIMIT_BYTES = 56 * 1024 * 1024
SUBLANES = 8

NORM_TM = 512
INPROJ_TM = 1024
INPROJ_TN = 1024
MIX_ROWS = CHUNK
OUT_TM = 256
FFN_IN_TM = 1024
FFN_TF = 512
FFN_OUT_TM = 256


def _compiler_params(semantics):
    return pltpu.CompilerParams(dimension_semantics=semantics,
                                vmem_limit_bytes=V7X_VMEM_LIMIT_BYTES)


def _rms_norm_f32(x, g):
    var = jnp.mean(x * x, axis=-1, keepdims=True)
    return x * jax.lax.rsqrt(var + EPS) * g


def _sigmoid(x):
    return 1.0 / (1.0 + jnp.exp(-x))


def _norm_kernel(x_ref, g_ref, o_ref):
    o_ref[...] = _rms_norm_f32(x_ref[...], g_ref[...]).astype(o_ref.dtype)


def _rms_norm(x, g, out_dtype):
    s, d = x.shape
    return pl.pallas_call(
        _norm_kernel,
        grid=(s // NORM_TM,),
        in_specs=[pl.BlockSpec((NORM_TM, d), lambda i: (i, 0)),
                  pl.BlockSpec((1, d), lambda i: (0, 0))],
        out_specs=pl.BlockSpec((NORM_TM, d), lambda i: (i, 0)),
        out_shape=jax.ShapeDtypeStruct((s, d), out_dtype),
        compiler_params=_compiler_params(("arbitrary",)),
        name="rms_norm",
    )(x, g.reshape(1, d))


def _in_proj_kernel(x_ref, w_ref, o_ref, wbf_ref):
    @pl.when(pl.program_id(1) == 0)
    def _():
        wbf_ref[...] = w_ref[...].astype(wbf_ref.dtype)

    o_ref[...] = jnp.dot(x_ref[...], wbf_ref[...],
                         preferred_element_type=jnp.float32).astype(o_ref.dtype)


def _in_proj(h, w, layer):
    s, d = h.shape
    n = w.shape[2]
    return pl.pallas_call(
        _in_proj_kernel,
        grid=(n // INPROJ_TN, s // INPROJ_TM),
        in_specs=[pl.BlockSpec((INPROJ_TM, d), lambda j, i: (i, 0)),
                  pl.BlockSpec((None, d, INPROJ_TN), lambda j, i: (layer, 0, j))],
        out_specs=pl.BlockSpec((INPROJ_TM, INPROJ_TN), lambda j, i: (i, j)),
        out_shape=jax.ShapeDtypeStruct((s, n), jnp.bfloat16),
        scratch_shapes=[pltpu.VMEM((d, INPROJ_TN), jnp.bfloat16)],
        compiler_params=_compiler_params(("arbitrary", "arbitrary")),
        name="in_proj",
    )(h, w)


def _mixer_kernel(b_ref, c_ref, u_ref, q_ref, k_ref, v0_ref, v1_ref, g0_ref, g1_ref,
                  wconv_ref, qcos_ref, qsin_ref, kcos_ref, ksin_ref,
                  mask_ref, qdec_ref, kdec_ref, cdec_ref,
                  bc_ref, gr_ref, ucbuf_ref, state_ref):
    step = pl.program_id(0)
    rows = MIX_ROWS

    @pl.when(step == 0)
    def _():
        ucbuf_ref[0:SUBLANES, :] = jnp.zeros((SUBLANES, CONV_WIDTH), jnp.float32)
        state_ref[...] = jnp.zeros_like(state_ref)

    uc = c_ref[...].astype(jnp.float32) * u_ref[...].astype(jnp.float32)
    ucbuf_ref[SUBLANES:SUBLANES + rows, :] = uc
    prev1 = ucbuf_ref[SUBLANES - 1:SUBLANES - 1 + rows, :]
    prev2 = ucbuf_ref[SUBLANES - 2:SUBLANES - 2 + rows, :]
    conv = (wconv_ref[0:1, :] * prev2 + wconv_ref[1:2, :] * prev1
            + wconv_ref[2:3, :] * uc)
    bc_ref[...] = (b_ref[...].astype(jnp.float32) * conv).astype(bc_ref.dtype)
    ucbuf_ref[0:SUBLANES, :] = uc[rows - SUBLANES:rows, :]

    qcos, qsin = qcos_ref[...], qsin_ref[...]
    kcos, ksin = kcos_ref[...], ksin_ref[...]
    half = RET_QK_DIM // 2
    for h in range(RET_HEADS):
        qk_cols = slice(h * RET_QK_DIM, (h + 1) * RET_QK_DIM)
        q = q_ref[:, qk_cols].astype(jnp.float32)
        k = k_ref[:, qk_cols].astype(jnp.float32)
        q = q * qcos + pltpu.roll(q, half, axis=1) * qsin
        k = k * kcos + pltpu.roll(k, half, axis=1) * ksin
        v_ref, g_ref = (v0_ref, g0_ref) if h < RET_HEADS // 2 else (v1_ref, g1_ref)
        hv = h % (RET_HEADS // 2)
        v_cols = slice(hv * RET_V_DIM, (hv + 1) * RET_V_DIM)
        v = v_ref[:, v_cols]
        qb = q.astype(jnp.bfloat16)
        kb = k.astype(jnp.bfloat16)
        scores = jax.lax.dot_general(qb, kb, (((1,), (1,)), ((), ())),
                                     preferred_element_type=jnp.float32)
        scores = scores * mask_ref[h]
        intra = jnp.dot(scores.astype(jnp.bfloat16), v,
                        preferred_element_type=jnp.float32)
        state = state_ref[h]
        inter = jnp.dot((q * qdec_ref[h]).astype(jnp.bfloat16),
                        state.astype(jnp.bfloat16),
                        preferred_element_type=jnp.float32)
        kd = (k * kdec_ref[h]).astype(jnp.bfloat16)
        kv = jax.lax.dot_general(kd, v, (((0,), (0,)), ((), ())),
                                 preferred_element_type=jnp.float32)
        state_ref[h] = cdec_ref[h] * state + kv
        o = intra + inter
        o = o * jax.lax.rsqrt(jnp.mean(o * o, axis=-1, keepdims=True) + EPS)
        g = g_ref[:, v_cols].astype(jnp.float32)
        out_cols = slice(h * RET_V_DIM, (h + 1) * RET_V_DIM)
        gr_ref[:, out_cols] = (g * _sigmoid(g) * o).astype(gr_ref.dtype)


def _retention_tables(seq):
    half = RET_QK_DIM // 2
    inv = ROPE_BASE ** (-jnp.arange(half, dtype=jnp.float32) / half)
    ang = jnp.arange(seq, dtype=jnp.float32)[:, None] * inv[None, :]
    cos, sin = jnp.cos(ang), jnp.sin(ang)
    cos2 = jnp.concatenate([cos, cos], axis=-1)
    sin2 = jnp.concatenate([-sin, sin], axis=-1)
    kscale = RET_QK_DIM ** -0.5
    log_gamma = jnp.log(1.0 - 2.0 ** (-5.0 - jnp.arange(RET_HEADS, dtype=jnp.float32)))
    idx = jnp.arange(CHUNK, dtype=jnp.float32)
    diff = idx[:, None] - idx[None, :]
    mask = jnp.where(diff[None] >= 0,
                     jnp.exp(log_gamma[:, None, None] * jnp.maximum(diff, 0.0)[None]), 0.0)
    q_decay = jnp.exp(log_gamma[:, None] * (idx + 1.0))
    k_decay = jnp.exp(log_gamma[:, None] * (CHUNK - 1.0 - idx))
    chunk_decay = jnp.exp(log_gamma * CHUNK)
    lanes = (RET_HEADS, CHUNK, RET_QK_DIM)
    return dict(
        qcos=cos2, qsin=sin2, kcos=cos2 * kscale, ksin=sin2 * kscale, mask=mask,
        qdec=jnp.broadcast_to(q_decay[:, :, None], lanes),
        kdec=jnp.broadcast_to(k_decay[:, :, None], lanes),
        cdec=jnp.broadcast_to(chunk_decay[:, None, None], (RET_HEADS, 1, RET_V_DIM)))


def _mixers(proj, w_conv, tables):
    s = proj.shape[0]
    rows = MIX_ROWS
    col = lambda c: pl.BlockSpec((rows, COL_BLOCK), lambda i, c=c: (i, c))
    rope = pl.BlockSpec((rows, RET_QK_DIM), lambda i: (i, 0))
    whole = lambda shape: pl.BlockSpec(shape, lambda i: (0,) * len(shape))
    return pl.pallas_call(
        _mixer_kernel,
        grid=(s // rows,),
        in_specs=[col(COL_B), col(COL_C), col(COL_U), col(COL_Q), col(COL_K),
                  col(COL_V0), col(COL_V1), col(COL_G0), col(COL_G1),
                  whole((CONV_K, CONV_WIDTH)),
                  rope, rope, rope, rope,
                  whole((RET_HEADS, CHUNK, CHUNK)),
                  whole((RET_HEADS, CHUNK, RET_QK_DIM)),
                  whole((RET_HEADS, CHUNK, RET_QK_DIM)),
                  whole((RET_HEADS, 1, RET_V_DIM))],
        out_specs=[pl.BlockSpec((rows, CONV_WIDTH), lambda i: (i, 0)),
                   pl.BlockSpec((rows, RET_VALUE_WIDTH), lambda i: (i, 0))],
        out_shape=[jax.ShapeDtypeStruct((s, CONV_WIDTH), jnp.bfloat16),
                   jax.ShapeDtypeStruct((s, RET_VALUE_WIDTH), jnp.bfloat16)],
        scratch_shapes=[pltpu.VMEM((SUBLANES + rows, CONV_WIDTH), jnp.float32),
                        pltpu.VMEM((RET_HEADS, RET_QK_DIM, RET_V_DIM), jnp.float32)],
        compiler_params=_compiler_params(("arbitrary",)),
        name="mixers",
    )(*([proj] * 9), w_conv,
      tables["qcos"], tables["qsin"], tables["kcos"], tables["ksin"],
      tables["mask"], tables["qdec"], tables["kdec"], tables["cdec"])


def _out_kernel(bc_ref, gr_ref, ac0_ref, ac1_ref, ar0_ref, ar1_ref, x_ref,
                wco_ref, wro_ref, wo_ref, g_ref, x1_ref, h2_ref):
    y_conv = jnp.dot(bc_ref[...], wco_ref[...], preferred_element_type=jnp.float32)
    y_ret = jnp.dot(gr_ref[...], wro_ref[...], preferred_element_type=jnp.float32)
    a_conv = jnp.concatenate([ac0_ref[...], ac1_ref[...]], axis=-1).astype(jnp.float32)
    a_ret = jnp.concatenate([ar0_ref[...], ar1_ref[...]], axis=-1).astype(jnp.float32)
    merged = _sigmoid(a_conv) * y_conv + _sigmoid(a_ret) * y_ret
    x1 = x_ref[...] + jnp.dot(merged.astype(jnp.bfloat16), wo_ref[...],
                              preferred_element_type=jnp.float32)
    x1_ref[...] = x1
    h2_ref[...] = _rms_norm_f32(x1, g_ref[...]).astype(h2_ref.dtype)


def _resident(shape, layer):
    return pl.BlockSpec((None,) + tuple(shape[1:]), lambda *_: (layer, 0, 0),
                        pipeline_mode=pl.Buffered(1))


def _out_proj(bc, gr, proj, x, w_co, w_ro, w_o, g_ffn, layer):
    s, d = x.shape
    tm = OUT_TM
    rows = lambda width: pl.BlockSpec((tm, width), lambda i: (i, 0))
    col = lambda c: pl.BlockSpec((tm, COL_BLOCK), lambda i, c=c: (i, c))
    return pl.pallas_call(
        _out_kernel,
        grid=(s // tm,),
        in_specs=[rows(CONV_WIDTH), rows(RET_VALUE_WIDTH),
                  col(COL_ACONV), col(COL_ACONV + 1), col(COL_ARET), col(COL_ARET + 1),
                  rows(d),
                  _resident(w_co.shape, layer), _resident(w_ro.shape, layer),
                  _resident(w_o.shape, layer), _resident(g_ffn.shape, layer)],
        out_specs=[rows(d), rows(d)],
        out_shape=[jax.ShapeDtypeStruct((s, d), jnp.float32),
                   jax.ShapeDtypeStruct((s, d), jnp.bfloat16)],
        compiler_params=_compiler_params(("arbitrary",)),
        name="out_proj",
    )(bc, gr, proj, proj, proj, proj, x, w_co, w_ro, w_o, g_ffn)


def _ffn_in_kernel(h2_ref, wg_ref, wu_ref, act_ref, wgbf_ref, wubf_ref):
    @pl.when(pl.program_id(1) == 0)
    def _():
        wgbf_ref[...] = wg_ref[...].astype(wgbf_ref.dtype)
        wubf_ref[...] = wu_ref[...].astype(wubf_ref.dtype)

    h2 = h2_ref[...]
    gate = jnp.dot(h2, wgbf_ref[...], preferred_element_type=jnp.float32)
    up = jnp.dot(h2, wubf_ref[...], preferred_element_type=jnp.float32)
    act_ref[...] = (gate * _sigmoid(gate) * up).astype(act_ref.dtype)


def _ffn_in(h2, w_in, layer):
    s, d = h2.shape
    tm, tf = FFN_IN_TM, FFN_TF
    nf = D_FF // tf
    return pl.pallas_call(
        _ffn_in_kernel,
        grid=(nf, s // tm),
        in_specs=[pl.BlockSpec((tm, d), lambda f, i: (i, 0)),
                  pl.BlockSpec((None, d, tf), lambda f, i: (layer, 0, f)),
                  pl.BlockSpec((None, d, tf), lambda f, i: (layer, 0, f + nf))],
        out_specs=pl.BlockSpec((tm, tf), lambda f, i: (i, f)),
        out_shape=jax.ShapeDtypeStruct((s, D_FF), jnp.bfloat16),
        scratch_shapes=[pltpu.VMEM((d, tf), jnp.bfloat16),
                        pltpu.VMEM((d, tf), jnp.bfloat16)],
        compiler_params=_compiler_params(("arbitrary", "arbitrary")),
        name="ffn_in",
    )(h2, w_in, w_in)


def _ffn_out_kernel(act_ref, x1_ref, wout_ref, g_ref, *out_refs, final):
    x2 = x1_ref[...] + jnp.dot(act_ref[...], wout_ref[...],
                               preferred_element_type=jnp.float32)
    normed = _rms_norm_f32(x2, g_ref[...])
    if final:
        out_refs[0][...] = normed
    else:
        out_refs[0][...] = x2
        out_refs[1][...] = normed.astype(out_refs[1].dtype)


def _ffn_out(act, x1, w_out, g_next, layer, final):
    s, d = x1.shape
    tm = FFN_OUT_TM
    rows = lambda width: pl.BlockSpec((tm, width), lambda i: (i, 0))
    out_specs = [rows(d)] if final else [rows(d), rows(d)]
    out_shape = [jax.ShapeDtypeStruct((s, d), jnp.float32)]
    if not final:
        out_shape.append(jax.ShapeDtypeStruct((s, d), jnp.bfloat16))
    return pl.pallas_call(
        functools.partial(_ffn_out_kernel, final=final),
        grid=(s // tm,),
        in_specs=[rows(D_FF), rows(d), _resident(w_out.shape, layer),
                  _resident(g_next.shape, layer)],
        out_specs=out_specs,
        out_shape=out_shape,
        compiler_params=_compiler_params(("arbitrary",)),
        name="ffn_out",
    )(act, x1, w_out, g_next)


def kernel(x, g_mix, w_in, w_conv, w_conv_out, w_ret_out, w_out, g_ffn, w_ffn_in,
           w_ffn_out, g_final):
    bsz, seq, d = x.shape
    depth = w_in.shape[0]
    assert bsz == 1 and d == D_MODEL and seq % INPROJ_TM == 0
    bf16 = jnp.bfloat16
    tables = _retention_tables(seq)
    w_co, w_ro, w_o = w_conv_out.astype(bf16), w_ret_out.astype(bf16), w_out.astype(bf16)
    w_fo = w_ffn_out.astype(bf16)
    g_next = jnp.concatenate([g_mix[1:], g_final[None]], axis=0).reshape(depth, 1, d)
    g_ffn3 = g_ffn.reshape(depth, 1, d)
    xs = x.reshape(seq, d)
    h = _rms_norm(xs, g_mix[0], bf16)
    for l in range(depth):
        proj = _in_proj(h, w_in, l)
        bc, gr = _mixers(proj, w_conv[l].reshape(CONV_K, CONV_WIDTH), tables)
        x1, h2 = _out_proj(bc, gr, proj, xs, w_co, w_ro, w_o, g_ffn3, l)
        act = _ffn_in(h2, w_ffn_in, l)
        if l == depth - 1:
            (xs,) = _ffn_out(act, x1, w_fo, g_next, l, True)
        else:
            xs, h = _ffn_out(act, x1, w_fo, g_next, l, False)
    return xs.reshape(bsz, seq, d)
```

```python
import functools

import jax
import jax.numpy as jnp
from jax.experimental import pallas as pl
from jax.experimental.pallas import tpu as pltpu

D_MODEL = 2048
CONV_WIDTH = D_MODEL // 2
CONV_K = 3
RET_HEADS = 8
RET_QK_DIM = D_MODEL // (2 * RET_HEADS)
RET_V_DIM = 2 * RET_QK_DIM
RET_KEY_WIDTH = RET_HEADS * RET_QK_DIM
RET_VALUE_WIDTH = RET_HEADS * RET_V_DIM
CHUNK = 128
ROPE_BASE = 10000.0
D_FF = ((8 * D_MODEL // 3 + 255) // 256) * 256
IN_WIDTH = 3 * CONV_WIDTH + 2 * RET_KEY_WIDTH + 2 * RET_VALUE_WIDTH + 2 * D_MODEL
EPS = 1e-6

COL_BLOCK = 1024
COL_B, COL_C, COL_U, COL_Q, COL_K = 0, 1, 2, 3, 4
COL_V0, COL_V1, COL_G0, COL_G1 = 5, 6, 7, 8
COL_ACONV = 9
COL_ARET = 11

V7X_VMEM_LIMIT_BYTES = 58 * 1024 * 1024
SUBLANES = 8

NORM_TM = 512
INPROJ_TM = 1024
INPROJ_TN = 1024
MIX_TM = 2 * CHUNK
PROJ_TN = 2 * D_MODEL // ((MIX_TM // CHUNK) * RET_HEADS)
FFN_IN_TM = 1024
FFN_TF = 512
FFN_OUT_TM = 256


def _compiler_params(semantics):
    return pltpu.CompilerParams(dimension_semantics=semantics,
                                vmem_limit_bytes=V7X_VMEM_LIMIT_BYTES)


def _rms_norm_f32(x, g):
    var = jnp.mean(x * x, axis=-1, keepdims=True)
    return x * jax.lax.rsqrt(var + EPS) * g


def _sigmoid(x):
    return 0.5 + 0.5 * jnp.tanh(0.5 * x)


def _silu(x):
    h = 0.5 * x
    return h + h * jnp.tanh(h)


def _resident(shape, layer):
    return pl.BlockSpec((None,) + tuple(shape[1:]), lambda *_: (layer, 0, 0),
                        pipeline_mode=pl.Buffered(1))


def _constant(shape):
    return pl.BlockSpec(tuple(shape), lambda *_: (0,) * len(shape),
                        pipeline_mode=pl.Buffered(1))


def _norm_kernel(x_ref, g_ref, o_ref):
    o_ref[...] = _rms_norm_f32(x_ref[...], g_ref[...]).astype(o_ref.dtype)


def _rms_norm(x, g, out_dtype):
    s, d = x.shape
    return pl.pallas_call(
        _norm_kernel,
        grid=(s // NORM_TM,),
        in_specs=[pl.BlockSpec((NORM_TM, d), lambda i: (i, 0)),
                  pl.BlockSpec((1, d), lambda i: (0, 0))],
        out_specs=pl.BlockSpec((NORM_TM, d), lambda i: (i, 0)),
        out_shape=jax.ShapeDtypeStruct((s, d), out_dtype),
        compiler_params=_compiler_params(("arbitrary",)),
        name="rms_norm",
    )(x, g.reshape(1, d))


def _in_proj_kernel(x_ref, w_ref, o_ref, wbf_ref):
    @pl.when(pl.program_id(1) == 0)
    def _():
        wbf_ref[...] = w_ref[...].astype(wbf_ref.dtype)

    o_ref[...] = jnp.dot(x_ref[...], wbf_ref[...],
                         preferred_element_type=jnp.float32).astype(o_ref.dtype)


def _in_proj(h, w, layer):
    s, d = h.shape
    n = w.shape[2]
    return pl.pallas_call(
        _in_proj_kernel,
        grid=(n // INPROJ_TN, s // INPROJ_TM),
        in_specs=[pl.BlockSpec((INPROJ_TM, d), lambda j, i: (i, 0)),
                  pl.BlockSpec((None, d, INPROJ_TN), lambda j, i: (layer, 0, j))],
        out_specs=pl.BlockSpec((INPROJ_TM, INPROJ_TN), lambda j, i: (i, j)),
        out_shape=jax.ShapeDtypeStruct((s, n), jnp.bfloat16),
        scratch_shapes=[pltpu.VMEM((d, INPROJ_TN), jnp.bfloat16)],
        compiler_params=_compiler_params(("arbitrary", "arbitrary")),
        name="in_proj",
    )(h, w)


def _mix_pieces(b_ref, c_ref, u_ref, q_ref, k_ref, v0_ref, v1_ref, g0_ref, g1_ref,
                wconv_ref, qcos_ref, qsin_ref, kcos_ref, ksin_ref,
                mask_ref, qdec_ref, kdec_ref, cdec_ref,
                bc_ref, gr_ref, ucbuf_ref, state_ref):
    rows = MIX_TM

    def conv_piece():
        uc = c_ref[...].astype(jnp.float32) * u_ref[...].astype(jnp.float32)
        ucbuf_ref[SUBLANES:SUBLANES + rows, :] = uc
        prev1 = ucbuf_ref[SUBLANES - 1:SUBLANES - 1 + rows, :]
        prev2 = ucbuf_ref[SUBLANES - 2:SUBLANES - 2 + rows, :]
        conv = (wconv_ref[0:1, :] * prev2 + wconv_ref[1:2, :] * prev1
                + wconv_ref[2:3, :] * uc)
        bc_ref[...] = (b_ref[...].astype(jnp.float32) * conv).astype(bc_ref.dtype)
        ucbuf_ref[0:SUBLANES, :] = uc[rows - SUBLANES:rows, :]

    def retention_piece(chunk, h):
        half = RET_QK_DIM // 2
        r = slice(chunk * CHUNK, (chunk + 1) * CHUNK)
        qk_cols = slice(h * RET_QK_DIM, (h + 1) * RET_QK_DIM)
        v_ref, g_ref = (v0_ref, g0_ref) if h < RET_HEADS // 2 else (v1_ref, g1_ref)
        hv = h % (RET_HEADS // 2)
        v_cols = slice(hv * RET_V_DIM, (hv + 1) * RET_V_DIM)
        t = {}

        def prepare():
            q = q_ref[r, qk_cols].astype(jnp.float32)
            k = k_ref[r, qk_cols].astype(jnp.float32)
            q = q * qcos_ref[r, :] + pltpu.roll(q, half, axis=1) * qsin_ref[r, :]
            k = k * kcos_ref[r, :] + pltpu.roll(k, half, axis=1) * ksin_ref[r, :]
            t["qb"] = q.astype(jnp.bfloat16)
            t["kb"] = k.astype(jnp.bfloat16)
            t["qd"] = (q * qdec_ref[h]).astype(jnp.bfloat16)
            t["kd"] = (k * kdec_ref[h]).astype(jnp.bfloat16)

        def first_matmuls():
            v = v_ref[r, v_cols]
            t["scores"] = jax.lax.dot_general(t["qb"], t["kb"], (((1,), (1,)), ((), ())),
                                              preferred_element_type=jnp.float32)
            t["inter"] = jnp.dot(t["qd"], state_ref[h].astype(jnp.bfloat16),
                                 preferred_element_type=jnp.float32)
            t["kv"] = jax.lax.dot_general(t["kd"], v, (((0,), (0,)), ((), ())),
                                          preferred_element_type=jnp.float32)

        def between():
            t["p"] = (t["scores"] * mask_ref[h]).astype(jnp.bfloat16)
            state_ref[h] = cdec_ref[h] * state_ref[h] + t["kv"]

        def second_matmul():
            t["intra"] = jnp.dot(t["p"], v_ref[r, v_cols], preferred_element_type=jnp.float32)

        def finish():
            o = t["intra"] + t["inter"]
            o = o * jax.lax.rsqrt(jnp.mean(o * o, axis=-1, keepdims=True) + EPS)
            g = g_ref[r, v_cols].astype(jnp.float32)
            out_cols = slice(h * RET_V_DIM, (h + 1) * RET_V_DIM)
            gr_ref[r, out_cols] = (_silu(g) * o).astype(gr_ref.dtype)

        return prepare, first_matmuls, between, second_matmul, finish

    retention = [retention_piece(chunk, h)
                 for chunk in range(rows // CHUNK) for h in range(RET_HEADS)]
    return conv_piece, retention


def _project_pieces(bc_ref, gr_ref, ac0_ref, ac1_ref, ar0_ref, ar1_ref, x_ref,
                    wco_ref, wro_ref, wo_ref, gain_ref, x1_ref, h2_ref, merged_ref):
    tn = PROJ_TN
    d = D_MODEL

    def merge_piece(n):
        cols = slice(n * tn, (n + 1) * tn)
        a_c, a_r = (ac0_ref, ar0_ref) if n * tn < COL_BLOCK else (ac1_ref, ar1_ref)
        a_cols = slice((n * tn) % COL_BLOCK, (n * tn) % COL_BLOCK + tn)
        y_conv = jnp.dot(bc_ref[...], wco_ref[:, cols], preferred_element_type=jnp.float32)
        y_ret = jnp.dot(gr_ref[...], wro_ref[:, cols], preferred_element_type=jnp.float32)
        merged = (_sigmoid(a_c[:, a_cols].astype(jnp.float32)) * y_conv
                  + _sigmoid(a_r[:, a_cols].astype(jnp.float32)) * y_ret)
        merged_ref[:, cols] = merged.astype(merged_ref.dtype)

    def residual_piece(n):
        cols = slice(n * tn, (n + 1) * tn)
        x1_ref[:, cols] = x_ref[:, cols] + jnp.dot(merged_ref[...], wo_ref[:, cols],
                                                   preferred_element_type=jnp.float32)

    def norm_piece():
        h2_ref[...] = _rms_norm_f32(x1_ref[...], gain_ref[...]).astype(h2_ref.dtype)

    return ([functools.partial(merge_piece, n) for n in range(d // tn)],
            [functools.partial(residual_piece, n) for n in range(d // tn)],
            norm_piece)


def _interleave(conv, retention, merges, residuals, norm):
    big = merges + residuals
    assert len(big) == len(retention)
    order = [retention[0][0]]
    for i, (_, first_matmuls, between, second_matmul, finish) in enumerate(retention):
        order.append(first_matmuls)
        if i + 1 < len(retention):
            order.append(retention[i + 1][0])
        order.append(big[i])
        if i == 0:
            order.append(conv)
        order += [between, second_matmul, finish]
    order.append(norm)
    return order


N_MIX_IN = 18
N_PROJ_IN = 9


def _mix_out_kernel(*refs):
    n_in = N_MIX_IN + N_PROJ_IN
    mix_in, proj_in, outs = refs[:N_MIX_IN], refs[N_MIX_IN:n_in], refs[n_in:n_in + 2]
    ucbuf_ref, state_ref, merged_ref, bc_a, gr_a, bc_b, gr_b = refs[n_in + 2:]
    step = pl.program_id(0)

    @pl.when(step == 0)
    def _():
        ucbuf_ref[0:SUBLANES, :] = jnp.zeros((SUBLANES, CONV_WIDTH), jnp.float32)
        state_ref[...] = jnp.zeros_like(state_ref)
        bc_b[...] = jnp.zeros_like(bc_b)
        gr_b[...] = jnp.zeros_like(gr_b)

    for parity, (bc_w, gr_w, bc_r, gr_r) in enumerate(
            [(bc_a, gr_a, bc_b, gr_b), (bc_b, gr_b, bc_a, gr_a)]):
        @pl.when(step % 2 == parity)
        def _(bc_w=bc_w, gr_w=gr_w, bc_r=bc_r, gr_r=gr_r):
            conv, retention = _mix_pieces(*mix_in, bc_w, gr_w, ucbuf_ref, state_ref)
            merges, residuals, norm = _project_pieces(bc_r, gr_r, *proj_in, *outs, merged_ref)
            for piece in _interleave(conv, retention, merges, residuals, norm):
                piece()


def _retention_tables(seq):
    half = RET_QK_DIM // 2
    inv = ROPE_BASE ** (-jnp.arange(half, dtype=jnp.float32) / half)
    ang = jnp.arange(seq, dtype=jnp.float32)[:, None] * inv[None, :]
    cos, sin = jnp.cos(ang), jnp.sin(ang)
    cos2 = jnp.concatenate([cos, cos], axis=-1)
    sin2 = jnp.concatenate([-sin, sin], axis=-1)
    kscale = RET_QK_DIM ** -0.5
    log_gamma = jnp.log(1.0 - 2.0 ** (-5.0 - jnp.arange(RET_HEADS, dtype=jnp.float32)))
    idx = jnp.arange(CHUNK, dtype=jnp.float32)
    diff = idx[:, None] - idx[None, :]
    mask = jnp.where(diff[None] >= 0,
                     jnp.exp(log_gamma[:, None, None] * jnp.maximum(diff, 0.0)[None]), 0.0)
    q_decay = jnp.exp(log_gamma[:, None] * (idx + 1.0))
    k_decay = jnp.exp(log_gamma[:, None] * (CHUNK - 1.0 - idx))
    chunk_decay = jnp.exp(log_gamma * CHUNK)
    lanes = (RET_HEADS, CHUNK, RET_QK_DIM)
    return dict(
        qcos=cos2, qsin=sin2, kcos=cos2 * kscale, ksin=sin2 * kscale, mask=mask,
        qdec=jnp.broadcast_to(q_decay[:, :, None], lanes),
        kdec=jnp.broadcast_to(k_decay[:, :, None], lanes),
        cdec=jnp.broadcast_to(chunk_decay[:, None, None], (RET_HEADS, 1, RET_V_DIM)))


def _mix_out(proj, x, w_conv, tables, w_co, w_ro, w_o, g_ffn, layer):
    s, d = x.shape
    tm = MIX_TM
    n_tiles = s // tm
    last = n_tiles - 1
    mixed = lambda i: jnp.minimum(i, last)
    projected = lambda i: jnp.maximum(i - 1, 0)
    mix_col = lambda c: pl.BlockSpec((tm, COL_BLOCK), lambda i, c=c: (mixed(i), c))
    rope = pl.BlockSpec((tm, RET_QK_DIM), lambda i: (mixed(i), 0))
    proj_col = lambda c: pl.BlockSpec((tm, COL_BLOCK), lambda i, c=c: (projected(i), c))
    proj_rows = pl.BlockSpec((tm, d), lambda i: (projected(i), 0))
    in_specs = [mix_col(COL_B), mix_col(COL_C), mix_col(COL_U), mix_col(COL_Q), mix_col(COL_K),
                mix_col(COL_V0), mix_col(COL_V1), mix_col(COL_G0), mix_col(COL_G1),
                _resident(w_conv.shape, layer),
                rope, rope, rope, rope,
                _constant(tables["mask"].shape), _constant(tables["qdec"].shape),
                _constant(tables["kdec"].shape), _constant(tables["cdec"].shape),
                proj_col(COL_ACONV), proj_col(COL_ACONV + 1),
                proj_col(COL_ARET), proj_col(COL_ARET + 1), proj_rows,
                _resident(w_co.shape, layer), _resident(w_ro.shape, layer),
                _resident(w_o.shape, layer), _resident(g_ffn.shape, layer)]
    assert len(in_specs) == N_MIX_IN + N_PROJ_IN
    handoff = [pltpu.VMEM((tm, CONV_WIDTH), jnp.bfloat16),
               pltpu.VMEM((tm, RET_VALUE_WIDTH), jnp.bfloat16)]
    return pl.pallas_call(
        _mix_out_kernel,
        grid=(n_tiles + 1,),
        in_specs=in_specs,
        out_specs=[proj_rows, proj_rows],
        out_shape=[jax.ShapeDtypeStruct((s, d), jnp.float32),
                   jax.ShapeDtypeStruct((s, d), jnp.bfloat16)],
        scratch_shapes=[pltpu.VMEM((SUBLANES + tm, CONV_WIDTH), jnp.float32),
                        pltpu.VMEM((RET_HEADS, RET_QK_DIM, RET_V_DIM), jnp.float32),
                        pltpu.VMEM((tm, d), jnp.bfloat16)]
                       + handoff + handoff,
        compiler_params=_compiler_params(("arbitrary",)),
        name="mix_out",
    )(*([proj] * 9), w_conv,
      tables["qcos"], tables["qsin"], tables["kcos"], tables["ksin"],
      tables["mask"], tables["qdec"], tables["kdec"], tables["cdec"],
      proj, proj, proj, proj, x, w_co, w_ro, w_o, g_ffn)


def _ffn_in_kernel(h2_ref, wg_ref, wu_ref, act_ref, wgbf_ref, wubf_ref):
    @pl.when(pl.program_id(1) == 0)
    def _():
        wgbf_ref[...] = wg_ref[...].astype(wgbf_ref.dtype)
        wubf_ref[...] = wu_ref[...].astype(wubf_ref.dtype)

    h2 = h2_ref[...]
    gate = jnp.dot(h2, wgbf_ref[...], preferred_element_type=jnp.float32)
    up = jnp.dot(h2, wubf_ref[...], preferred_element_type=jnp.float32)
    act_ref[...] = (_silu(gate) * up).astype(act_ref.dtype)


def _ffn_in(h2, w_in, layer):
    s, d = h2.shape
    tm, tf = FFN_IN_TM, FFN_TF
    nf = D_FF // tf
    return pl.pallas_call(
        _ffn_in_kernel,
        grid=(nf, s // tm),
        in_specs=[pl.BlockSpec((tm, d), lambda f, i: (i, 0)),
                  pl.BlockSpec((None, d, tf), lambda f, i: (layer, 0, f)),
                  pl.BlockSpec((None, d, tf), lambda f, i: (layer, 0, f + nf))],
        out_specs=pl.BlockSpec((tm, tf), lambda f, i: (i, f)),
        out_shape=jax.ShapeDtypeStruct((s, D_FF), jnp.bfloat16),
        scratch_shapes=[pltpu.VMEM((d, tf), jnp.bfloat16),
                        pltpu.VMEM((d, tf), jnp.bfloat16)],
        compiler_params=_compiler_params(("arbitrary", "arbitrary")),
        name="ffn_in",
    )(h2, w_in, w_in)


def _ffn_out_kernel(act_ref, x1_ref, wout_ref, g_ref, *out_refs, final):
    x2 = x1_ref[...] + jnp.dot(act_ref[...], wout_ref[...],
                               preferred_element_type=jnp.float32)
    normed = _rms_norm_f32(x2, g_ref[...])
    if final:
        out_refs[0][...] = normed
    else:
        out_refs[0][...] = x2
        out_refs[1][...] = normed.astype(out_refs[1].dtype)


def _ffn_out(act, x1, w_out, g_next, layer, final):
    s, d = x1.shape
    tm = FFN_OUT_TM
    rows = lambda width: pl.BlockSpec((tm, width), lambda i: (i, 0))
    out_specs = [rows(d)] if final else [rows(d), rows(d)]
    out_shape = [jax.ShapeDtypeStruct((s, d), jnp.float32)]
    if not final:
        out_shape.append(jax.ShapeDtypeStruct((s, d), jnp.bfloat16))
    return pl.pallas_call(
        functools.partial(_ffn_out_kernel, final=final),
        grid=(s // tm,),
        in_specs=[rows(D_FF), rows(d), _resident(w_out.shape, layer),
                  _resident(g_next.shape, layer)],
        out_specs=out_specs,
        out_shape=out_shape,
        compiler_params=_compiler_params(("arbitrary",)),
        name="ffn_out",
    )(act, x1, w_out, g_next)


def kernel(x, g_mix, w_in, w_conv, w_conv_out, w_ret_out, w_out, g_ffn, w_ffn_in,
           w_ffn_out, g_final):
    bsz, seq, d = x.shape
    depth = w_in.shape[0]
    assert bsz == 1 and d == D_MODEL and seq % INPROJ_TM == 0
    bf16 = jnp.bfloat16
    tables = _retention_tables(seq)
    w_co, w_ro, w_o = w_conv_out.astype(bf16), w_ret_out.astype(bf16), w_out.astype(bf16)
    w_fo = w_ffn_out.astype(bf16)
    w_cv = w_conv.reshape(depth, CONV_K, CONV_WIDTH)
    g_next = jnp.concatenate([g_mix[1:], g_final[None]], axis=0).reshape(depth, 1, d)
    g_ffn3 = g_ffn.reshape(depth, 1, d)
    xs = x.reshape(seq, d)
    h = _rms_norm(xs, g_mix[0], bf16)
    for l in range(depth):
        proj = _in_proj(h, w_in, l)
        x1, h2 = _mix_out(proj, xs, w_cv, tables, w_co, w_ro, w_o, g_ffn3, l)
        act = _ffn_in(h2, w_ffn_in, l)
        if l == depth - 1:
            (xs,) = _ffn_out(act, x1, w_fo, g_next, l, True)
        else:
            xs, h = _ffn_out(act, x1, w_fo, g_next, l, False)
    return xs.reshape(bsz, seq, d)
```

```python
import functools

import jax
import jax.numpy as jnp
from jax.experimental import pallas as pl
from jax.experimental.pallas import tpu as pltpu

D_MODEL = 2048
CONV_WIDTH = D_MODEL // 2
CONV_K = 3
RET_HEADS = 8
RET_QK_DIM = D_MODEL // (2 * RET_HEADS)
RET_V_DIM = 2 * RET_QK_DIM
RET_KEY_WIDTH = RET_HEADS * RET_QK_DIM
RET_VALUE_WIDTH = RET_HEADS * RET_V_DIM
CHUNK = 128
ROPE_BASE = 10000.0
D_FF = ((8 * D_MODEL // 3 + 255) // 256) * 256
IN_WIDTH = 3 * CONV_WIDTH + 2 * RET_KEY_WIDTH + 2 * RET_VALUE_WIDTH + 2 * D_MODEL
EPS = 1e-6

COL_BLOCK = 1024
COL_B, COL_C, COL_U, COL_Q, COL_K = 0, 1, 2, 3, 4
COL_V0, COL_V1, COL_G0, COL_G1 = 5, 6, 7, 8
COL_ACONV = 9
COL_ARET = 11

V7X_VMEM_LIMIT_BYTES = 58 * 1024 * 1024
SUBLANES = 8

NORM_TM = 512
INPROJ_TM = 2048
INPROJ_TN = 1024
MIX_TM = 2 * CHUNK
PROJ_TN = 2 * D_MODEL // ((MIX_TM // CHUNK) * RET_HEADS)
FFN_IN_TM = 2048
FFN_TF = 512
FFN_IN_SUB_M = 512
FFN_IN_SUB_N = 256
FFN_OUT_TM = 512
FFN_OUT_SUB_M = 256


def _compiler_params(semantics):
    return pltpu.CompilerParams(dimension_semantics=semantics,
                                vmem_limit_bytes=V7X_VMEM_LIMIT_BYTES)


def _rms_norm_f32(x, g):
    var = jnp.mean(x * x, axis=-1, keepdims=True)
    return x * jax.lax.rsqrt(var + EPS) * g


def _sigmoid(x):
    return 0.5 + 0.5 * jnp.tanh(0.5 * x)


def _silu(x):
    h = 0.5 * x
    return h + h * jnp.tanh(h)


def _resident(shape, layer):
    return pl.BlockSpec((None,) + tuple(shape[1:]), lambda *_: (layer, 0, 0),
                        pipeline_mode=pl.Buffered(1))


def _constant(shape):
    return pl.BlockSpec(tuple(shape), lambda *_: (0,) * len(shape),
                        pipeline_mode=pl.Buffered(1))


def _norm_kernel(x_ref, g_ref, o_ref):
    o_ref[...] = _rms_norm_f32(x_ref[...], g_ref[...]).astype(o_ref.dtype)


def _rms_norm(x, g, out_dtype):
    s, d = x.shape
    return pl.pallas_call(
        _norm_kernel,
        grid=(s // NORM_TM,),
        in_specs=[pl.BlockSpec((NORM_TM, d), lambda i: (i, 0)),
                  pl.BlockSpec((1, d), lambda i: (0, 0))],
        out_specs=pl.BlockSpec((NORM_TM, d), lambda i: (i, 0)),
        out_shape=jax.ShapeDtypeStruct((s, d), out_dtype),
        compiler_params=_compiler_params(("arbitrary",)),
        name="rms_norm",
    )(x, g.reshape(1, d))


def _in_proj_kernel(x_ref, w_ref, o_ref, wbf_ref):
    @pl.when(pl.program_id(1) == 0)
    def _():
        wbf_ref[...] = w_ref[...].astype(wbf_ref.dtype)

    o_ref[...] = jnp.dot(x_ref[...], wbf_ref[...],
                         preferred_element_type=jnp.float32).astype(o_ref.dtype)


def _in_proj(h, w, layer):
    s, d = h.shape
    n = w.shape[2]
    return pl.pallas_call(
        _in_proj_kernel,
        grid=(n // INPROJ_TN, s // INPROJ_TM),
        in_specs=[pl.BlockSpec((INPROJ_TM, d), lambda j, i: (i, 0)),
                  pl.BlockSpec((None, d, INPROJ_TN), lambda j, i: (layer, 0, j))],
        out_specs=pl.BlockSpec((INPROJ_TM, INPROJ_TN), lambda j, i: (i, j)),
        out_shape=jax.ShapeDtypeStruct((s, n), jnp.bfloat16),
        scratch_shapes=[pltpu.VMEM((d, INPROJ_TN), jnp.bfloat16)],
        compiler_params=_compiler_params(("arbitrary", "arbitrary")),
        name="in_proj",
    )(h, w)


def _mix_pieces(b_ref, c_ref, u_ref, q_ref, k_ref, v0_ref, v1_ref, g0_ref, g1_ref,
                wconv_ref, tile_cos_ref, tile_sin_ref, row_cos_ref, row_sin_ref,
                row_cos_signed_ref, row_sin_signed_ref,
                mask_ref, qdec_ref, kdec_ref, cdec_ref,
                bc_ref, gr_ref, ucbuf_ref, state_ref, rope_ref):
    rows = MIX_TM
    kscale = RET_QK_DIM ** -0.5

    def rope_piece():
        ca, sa = tile_cos_ref[...], tile_sin_ref[...]
        rope_ref[0] = ca * row_cos_ref[...] - sa * row_sin_ref[...]
        rope_ref[1] = sa * row_cos_signed_ref[...] + ca * row_sin_signed_ref[...]

    def conv_piece():
        uc = c_ref[...].astype(jnp.float32) * u_ref[...].astype(jnp.float32)
        ucbuf_ref[SUBLANES:SUBLANES + rows, :] = uc
        prev1 = ucbuf_ref[SUBLANES - 1:SUBLANES - 1 + rows, :]
        prev2 = ucbuf_ref[SUBLANES - 2:SUBLANES - 2 + rows, :]
        conv = (wconv_ref[0:1, :] * prev2 + wconv_ref[1:2, :] * prev1
                + wconv_ref[2:3, :] * uc)
        bc_ref[...] = (b_ref[...].astype(jnp.float32) * conv).astype(bc_ref.dtype)
        ucbuf_ref[0:SUBLANES, :] = uc[rows - SUBLANES:rows, :]

    def retention_piece(chunk, h):
        half = RET_QK_DIM // 2
        r = slice(chunk * CHUNK, (chunk + 1) * CHUNK)
        qk_cols = slice(h * RET_QK_DIM, (h + 1) * RET_QK_DIM)
        v_ref, g_ref = (v0_ref, g0_ref) if h < RET_HEADS // 2 else (v1_ref, g1_ref)
        hv = h % (RET_HEADS // 2)
        v_cols = slice(hv * RET_V_DIM, (hv + 1) * RET_V_DIM)
        t = {}

        def prepare():
            q = q_ref[r, qk_cols].astype(jnp.float32)
            k = k_ref[r, qk_cols].astype(jnp.float32)
            cos2, sin2 = rope_ref[0, r, :], rope_ref[1, r, :]
            q = q * cos2 + pltpu.roll(q, half, axis=1) * sin2
            k = (k * cos2 + pltpu.roll(k, half, axis=1) * sin2) * kscale
            t["qb"] = q.astype(jnp.bfloat16)
            t["kb"] = k.astype(jnp.bfloat16)
            t["qd"] = (q * qdec_ref[h]).astype(jnp.bfloat16)
            t["kd"] = (k * kdec_ref[h]).astype(jnp.bfloat16)

        def first_matmuls():
            v = v_ref[r, v_cols]
            t["scores"] = jax.lax.dot_general(t["qb"], t["kb"], (((1,), (1,)), ((), ())),
                                              preferred_element_type=jnp.float32)
            t["inter"] = jnp.dot(t["qd"], state_ref[h].astype(jnp.bfloat16),
                                 preferred_element_type=jnp.float32)
            t["kv"] = jax.lax.dot_general(t["kd"], v, (((0,), (0,)), ((), ())),
                                          preferred_element_type=jnp.float32)

        def between():
            t["p"] = (t["scores"] * mask_ref[h]).astype(jnp.bfloat16)
            state_ref[h] = cdec_ref[h] * state_ref[h] + t["kv"]

        def second_matmul():
            t["intra"] = jnp.dot(t["p"], v_ref[r, v_cols], preferred_element_type=jnp.float32)

        def finish():
            o = t["intra"] + t["inter"]
            o = o * jax.lax.rsqrt(jnp.mean(o * o, axis=-1, keepdims=True) + EPS)
            g = g_ref[r, v_cols].astype(jnp.float32)
            out_cols = slice(h * RET_V_DIM, (h + 1) * RET_V_DIM)
            gr_ref[r, out_cols] = (_silu(g) * o).astype(gr_ref.dtype)

        return prepare, first_matmuls, between, second_matmul, finish

    retention = [retention_piece(chunk, h)
                 for chunk in range(rows // CHUNK) for h in range(RET_HEADS)]
    return rope_piece, conv_piece, retention


def _project_pieces(bc_ref, gr_ref, ac0_ref, ac1_ref, ar0_ref, ar1_ref, x_ref,
                    wco_ref, wro_ref, wo_ref, gain_ref, x1_ref, h2_ref, merged_ref):
    tn = PROJ_TN
    d = D_MODEL

    def merge_piece(n):
        cols = slice(n * tn, (n + 1) * tn)
        a_c, a_r = (ac0_ref, ar0_ref) if n * tn < COL_BLOCK else (ac1_ref, ar1_ref)
        a_cols = slice((n * tn) % COL_BLOCK, (n * tn) % COL_BLOCK + tn)
        y_conv = jnp.dot(bc_ref[...], wco_ref[:, cols], preferred_element_type=jnp.float32)
        y_ret = jnp.dot(gr_ref[...], wro_ref[:, cols], preferred_element_type=jnp.float32)
        merged = (_sigmoid(a_c[:, a_cols].astype(jnp.float32)) * y_conv
                  + _sigmoid(a_r[:, a_cols].astype(jnp.float32)) * y_ret)
        merged_ref[:, cols] = merged.astype(merged_ref.dtype)

    def residual_piece(n):
        cols = slice(n * tn, (n + 1) * tn)
        x1_ref[:, cols] = x_ref[:, cols] + jnp.dot(merged_ref[...], wo_ref[:, cols],
                                                   preferred_element_type=jnp.float32)

    def norm_piece():
        h2_ref[...] = _rms_norm_f32(x1_ref[...], gain_ref[...]).astype(h2_ref.dtype)

    return ([functools.partial(merge_piece, n) for n in range(d // tn)],
            [functools.partial(residual_piece, n) for n in range(d // tn)],
            norm_piece)


def _interleave(rope, conv, retention, merges, residuals, norm):
    big = merges + residuals
    assert len(big) == len(retention)
    order = [rope, retention[0][0]]
    for i, (_, first_matmuls, between, second_matmul, finish) in enumerate(retention):
        order.append(first_matmuls)
        if i + 1 < len(retention):
            order.append(retention[i + 1][0])
        order.append(big[i])
        if i == 0:
            order.append(conv)
        order += [between, second_matmul, finish]
    order.append(norm)
    return order


N_MIX_IN = 20
N_PROJ_IN = 9


def _mix_out_kernel(*refs):
    n_in = N_MIX_IN + N_PROJ_IN
    mix_in, proj_in, outs = refs[:N_MIX_IN], refs[N_MIX_IN:n_in], refs[n_in:n_in + 2]
    ucbuf_ref, state_ref, rope_ref, merged_ref, bc_a, gr_a, bc_b, gr_b = refs[n_in + 2:]
    step = pl.program_id(0)

    @pl.when(step == 0)
    def _():
        ucbuf_ref[0:SUBLANES, :] = jnp.zeros((SUBLANES, CONV_WIDTH), jnp.float32)
        state_ref[...] = jnp.zeros_like(state_ref)
        bc_b[...] = jnp.zeros_like(bc_b)
        gr_b[...] = jnp.zeros_like(gr_b)

    for parity, (bc_w, gr_w, bc_r, gr_r) in enumerate(
            [(bc_a, gr_a, bc_b, gr_b), (bc_b, gr_b, bc_a, gr_a)]):
        @pl.when(step % 2 == parity)
        def _(bc_w=bc_w, gr_w=gr_w, bc_r=bc_r, gr_r=gr_r):
            rope, conv, retention = _mix_pieces(*mix_in, bc_w, gr_w, ucbuf_ref, state_ref,
                                                rope_ref)
            merges, residuals, norm = _project_pieces(bc_r, gr_r, *proj_in, *outs, merged_ref)
            for piece in _interleave(rope, conv, retention, merges, residuals, norm):
                piece()


def _retention_tables(seq):
    half = RET_QK_DIM // 2
    inv = ROPE_BASE ** (-jnp.arange(half, dtype=jnp.float32) / half)
    tile_ang = (jnp.arange(seq // MIX_TM, dtype=jnp.float32) * MIX_TM)[:, None] * inv[None, :]
    row_ang = jnp.arange(MIX_TM, dtype=jnp.float32)[:, None] * inv[None, :]
    both_halves = lambda t: jnp.concatenate([t, t], axis=-1)
    sign = jnp.concatenate([-jnp.ones((half,), jnp.float32), jnp.ones((half,), jnp.float32)])
    row_cos, row_sin = both_halves(jnp.cos(row_ang)), both_halves(jnp.sin(row_ang))
    log_gamma = jnp.log(1.0 - 2.0 ** (-5.0 - jnp.arange(RET_HEADS, dtype=jnp.float32)))
    idx = jnp.arange(CHUNK, dtype=jnp.float32)
    diff = idx[:, None] - idx[None, :]
    mask = jnp.where(diff[None] >= 0,
                     jnp.exp(log_gamma[:, None, None] * jnp.maximum(diff, 0.0)[None]), 0.0)
    q_decay = jnp.exp(log_gamma[:, None] * (idx + 1.0))
    k_decay = jnp.exp(log_gamma[:, None] * (CHUNK - 1.0 - idx))
    chunk_decay = jnp.exp(log_gamma * CHUNK)
    lanes = (RET_HEADS, CHUNK, RET_QK_DIM)
    return dict(
        tile_cos=both_halves(jnp.cos(tile_ang))[:, None, :],
        tile_sin=both_halves(jnp.sin(tile_ang))[:, None, :],
        row_cos=row_cos, row_sin=row_sin,
        row_cos_signed=row_cos * sign, row_sin_signed=row_sin * sign, mask=mask,
        qdec=jnp.broadcast_to(q_decay[:, :, None], lanes),
        kdec=jnp.broadcast_to(k_decay[:, :, None], lanes),
        cdec=jnp.broadcast_to(chunk_decay[:, None, None], (RET_HEADS, 1, RET_V_DIM)))


def _mix_out(proj, x, w_conv, tables, w_co, w_ro, w_o, g_ffn, layer):
    s, d = x.shape
    tm = MIX_TM
    n_tiles = s // tm
    last = n_tiles - 1
    mixed = lambda i: jnp.minimum(i, last)
    projected = lambda i: jnp.maximum(i - 1, 0)
    mix_col = lambda c: pl.BlockSpec((tm, COL_BLOCK), lambda i, c=c: (mixed(i), c))
    tile_rope = pl.BlockSpec((None, 1, RET_QK_DIM), lambda i: (mixed(i), 0, 0))
    row_rope = _constant((tm, RET_QK_DIM))
    proj_col = lambda c: pl.BlockSpec((tm, COL_BLOCK), lambda i, c=c: (projected(i), c))
    proj_rows = pl.BlockSpec((tm, d), lambda i: (projected(i), 0))
    in_specs = [mix_col(COL_B), mix_col(COL_C), mix_col(COL_U), mix_col(COL_Q), mix_col(COL_K),
                mix_col(COL_V0), mix_col(COL_V1), mix_col(COL_G0), mix_col(COL_G1),
                _resident(w_conv.shape, layer),
                tile_rope, tile_rope, row_rope, row_rope, row_rope, row_rope,
                _constant(tables["mask"].shape), _constant(tables["qdec"].shape),
                _constant(tables["kdec"].shape), _constant(tables["cdec"].shape),
                proj_col(COL_ACONV), proj_col(COL_ACONV + 1),
                proj_col(COL_ARET), proj_col(COL_ARET + 1), proj_rows,
                _resident(w_co.shape, layer), _resident(w_ro.shape, layer),
                _resident(w_o.shape, layer), _resident(g_ffn.shape, layer)]
    assert len(in_specs) == N_MIX_IN + N_PROJ_IN
    handoff = [pltpu.VMEM((tm, CONV_WIDTH), jnp.bfloat16),
               pltpu.VMEM((tm, RET_VALUE_WIDTH), jnp.bfloat16)]
    return pl.pallas_call(
        _mix_out_kernel,
        grid=(n_tiles + 1,),
        in_specs=in_specs,
        out_specs=[proj_rows, proj_rows],
        out_shape=[jax.ShapeDtypeStruct((s, d), jnp.float32),
                   jax.ShapeDtypeStruct((s, d), jnp.bfloat16)],
        scratch_shapes=[pltpu.VMEM((SUBLANES + tm, CONV_WIDTH), jnp.float32),
                        pltpu.VMEM((RET_HEADS, RET_QK_DIM, RET_V_DIM), jnp.float32),
                        pltpu.VMEM((2, tm, RET_QK_DIM), jnp.float32),
                        pltpu.VMEM((tm, d), jnp.bfloat16)]
                       + handoff + handoff,
        compiler_params=_compiler_params(("arbitrary",)),
        name="mix_out",
    )(*([proj] * 9), w_conv,
      tables["tile_cos"], tables["tile_sin"], tables["row_cos"], tables["row_sin"],
      tables["row_cos_signed"], tables["row_sin_signed"],
      tables["mask"], tables["qdec"], tables["kdec"], tables["cdec"],
      proj, proj, proj, proj, x, w_co, w_ro, w_o, g_ffn)


def _ffn_in_kernel(h2_ref, wg_ref, wu_ref, act_ref, wgbf_ref, wubf_ref):
    @pl.when(pl.program_id(1) == 0)
    def _():
        wgbf_ref[...] = wg_ref[...].astype(wgbf_ref.dtype)
        wubf_ref[...] = wu_ref[...].astype(wubf_ref.dtype)

    tm, tf = act_ref.shape
    for m in range(tm // FFN_IN_SUB_M):
        rows = slice(m * FFN_IN_SUB_M, (m + 1) * FFN_IN_SUB_M)
        h2 = h2_ref[rows, :]
        for n in range(tf // FFN_IN_SUB_N):
            cols = slice(n * FFN_IN_SUB_N, (n + 1) * FFN_IN_SUB_N)
            gate = jnp.dot(h2, wgbf_ref[:, cols], preferred_element_type=jnp.float32)
            up = jnp.dot(h2, wubf_ref[:, cols], preferred_element_type=jnp.float32)
            act_ref[rows, cols] = (_silu(gate) * up).astype(act_ref.dtype)


def _ffn_in(h2, w_in, layer):
    s, d = h2.shape
    tm, tf = FFN_IN_TM, FFN_TF
    nf = D_FF // tf
    return pl.pallas_call(
        _ffn_in_kernel,
        grid=(nf, s // tm),
        in_specs=[pl.BlockSpec((tm, d), lambda f, i: (i, 0)),
                  pl.BlockSpec((None, d, tf), lambda f, i: (layer, 0, f)),
                  pl.BlockSpec((None, d, tf), lambda f, i: (layer, 0, f + nf))],
        out_specs=pl.BlockSpec((tm, tf), lambda f, i: (i, f)),
        out_shape=jax.ShapeDtypeStruct((s, D_FF), jnp.bfloat16),
        scratch_shapes=[pltpu.VMEM((d, tf), jnp.bfloat16),
                        pltpu.VMEM((d, tf), jnp.bfloat16)],
        compiler_params=_compiler_params(("arbitrary", "arbitrary")),
        name="ffn_in",
    )(h2, w_in, w_in)


def _ffn_out_kernel(act_ref, x1_ref, wout_ref, g_ref, *out_refs, final):
    for m in range(act_ref.shape[0] // FFN_OUT_SUB_M):
        rows = slice(m * FFN_OUT_SUB_M, (m + 1) * FFN_OUT_SUB_M)
        x2 = x1_ref[rows, :] + jnp.dot(act_ref[rows, :], wout_ref[...],
                                       preferred_element_type=jnp.float32)
        normed = _rms_norm_f32(x2, g_ref[...])
        if final:
            out_refs[0][rows, :] = normed
        else:
            out_refs[0][rows, :] = x2
            out_refs[1][rows, :] = normed.astype(out_refs[1].dtype)


def _ffn_out(act, x1, w_out, g_next, layer, final):
    s, d = x1.shape
    tm = FFN_OUT_TM
    rows = lambda width: pl.BlockSpec((tm, width), lambda i: (i, 0))
    out_specs = [rows(d)] if final else [rows(d), rows(d)]
    out_shape = [jax.ShapeDtypeStruct((s, d), jnp.float32)]
    if not final:
        out_shape.append(jax.ShapeDtypeStruct((s, d), jnp.bfloat16))
    return pl.pallas_call(
        functools.partial(_ffn_out_kernel, final=final),
        grid=(s // tm,),
        in_specs=[rows(D_FF), rows(d), _resident(w_out.shape, layer),
                  _resident(g_next.shape, layer)],
        out_specs=out_specs,
        out_shape=out_shape,
        compiler_params=_compiler_params(("arbitrary",)),
        name="ffn_out",
    )(act, x1, w_out, g_next)


def kernel(x, g_mix, w_in, w_conv, w_conv_out, w_ret_out, w_out, g_ffn, w_ffn_in,
           w_ffn_out, g_final):
    bsz, seq, d = x.shape
    depth = w_in.shape[0]
    assert bsz == 1 and d == D_MODEL and seq % INPROJ_TM == 0
    bf16 = jnp.bfloat16
    tables = _retention_tables(seq)
    w_co, w_ro, w_o = w_conv_out.astype(bf16), w_ret_out.astype(bf16), w_out.astype(bf16)
    w_fo = w_ffn_out.astype(bf16)
    w_cv = w_conv.reshape(depth, CONV_K, CONV_WIDTH)
    g_next = jnp.concatenate([g_mix[1:], g_final[None]], axis=0).reshape(depth, 1, d)
    g_ffn3 = g_ffn.reshape(depth, 1, d)
    xs = x.reshape(seq, d)
    h = _rms_norm(xs, g_mix[0], bf16)
    for l in range(depth):
        proj = _in_proj(h, w_in, l)
        x1, h2 = _mix_out(proj, xs, w_cv, tables, w_co, w_ro, w_o, g_ffn3, l)
        act = _ffn_in(h2, w_ffn_in, l)
        if l == depth - 1:
            (xs,) = _ffn_out(act, x1, w_fo, g_next, l, True)
        else:
            xs, h = _ffn_out(act, x1, w_fo, g_next, l, False)
    return xs.reshape(bsz, seq, d)
```

```python
import functools

import jax
import jax.numpy as jnp
from jax.experimental import pallas as pl
from jax.experimental.pallas import tpu as pltpu

D_MODEL = 2048
CONV_WIDTH = D_MODEL // 2
CONV_K = 3
RET_HEADS = 8
RET_QK_DIM = D_MODEL // (2 * RET_HEADS)
RET_V_DIM = 2 * RET_QK_DIM
RET_KEY_WIDTH = RET_HEADS * RET_QK_DIM
RET_VALUE_WIDTH = RET_HEADS * RET_V_DIM
CHUNK = 128
ROPE_BASE = 10000.0
D_FF = ((8 * D_MODEL // 3 + 255) // 256) * 256
IN_WIDTH = 3 * CONV_WIDTH + 2 * RET_KEY_WIDTH + 2 * RET_VALUE_WIDTH + 2 * D_MODEL
EPS = 1e-6

COL_BLOCK = 1024
COL_B, COL_C, COL_U, COL_Q, COL_K = 0, 1, 2, 3, 4
COL_V0, COL_V1, COL_G0, COL_G1 = 5, 6, 7, 8
COL_ACONV = 9
COL_ARET = 11

V7X_VMEM_LIMIT_BYTES = 58 * 1024 * 1024
SUBLANES = 8

NORM_TM = 512
INPROJ_TM = 2048
INPROJ_TN = 1024
MIX_TM = 2 * CHUNK
CONV_TN = 128
PROJ_TN = 2 * D_MODEL // ((MIX_TM // CHUNK) * RET_HEADS)
FFN_IN_TM = 2048
FFN_TF = 512
FFN_IN_SUB_M = 512
FFN_IN_SUB_N = 256
FFN_OUT_TM = 512
FFN_OUT_SUB_M = 256


def _compiler_params(semantics):
    return pltpu.CompilerParams(dimension_semantics=semantics,
                                vmem_limit_bytes=V7X_VMEM_LIMIT_BYTES)


def _rms_norm_f32(x, g):
    var = jnp.mean(x * x, axis=-1, keepdims=True)
    return x * jax.lax.rsqrt(var + EPS) * g


def _sigmoid(x):
    return 0.5 + 0.5 * jnp.tanh(0.5 * x)


def _silu(x):
    h = 0.5 * x
    return h + h * jnp.tanh(h)


def _resident(shape, layer):
    return pl.BlockSpec((None,) + tuple(shape[1:]), lambda *_: (layer, 0, 0),
                        pipeline_mode=pl.Buffered(1))


def _constant(shape):
    return pl.BlockSpec(tuple(shape), lambda *_: (0,) * len(shape),
                        pipeline_mode=pl.Buffered(1))


def _norm_kernel(x_ref, g_ref, o_ref):
    o_ref[...] = _rms_norm_f32(x_ref[...], g_ref[...]).astype(o_ref.dtype)


def _rms_norm(x, g, out_dtype):
    s, d = x.shape
    return pl.pallas_call(
        _norm_kernel,
        grid=(s // NORM_TM,),
        in_specs=[pl.BlockSpec((NORM_TM, d), lambda i: (i, 0)),
                  pl.BlockSpec((1, d), lambda i: (0, 0))],
        out_specs=pl.BlockSpec((NORM_TM, d), lambda i: (i, 0)),
        out_shape=jax.ShapeDtypeStruct((s, d), out_dtype),
        compiler_params=_compiler_params(("arbitrary",)),
        name="rms_norm",
    )(x, g.reshape(1, d))


def _in_proj_kernel(x_ref, w_ref, o_ref, wbf_ref):
    @pl.when(pl.program_id(1) == 0)
    def _():
        wbf_ref[...] = w_ref[...].astype(wbf_ref.dtype)

    o_ref[...] = jnp.dot(x_ref[...], wbf_ref[...],
                         preferred_element_type=jnp.float32).astype(o_ref.dtype)


def _in_proj(h, w, layer):
    s, d = h.shape
    n = w.shape[2]
    return pl.pallas_call(
        _in_proj_kernel,
        grid=(n // INPROJ_TN, s // INPROJ_TM),
        in_specs=[pl.BlockSpec((INPROJ_TM, d), lambda j, i: (i, 0)),
                  pl.BlockSpec((None, d, INPROJ_TN), lambda j, i: (layer, 0, j))],
        out_specs=pl.BlockSpec((INPROJ_TM, INPROJ_TN), lambda j, i: (i, j)),
        out_shape=jax.ShapeDtypeStruct((s, n), jnp.bfloat16),
        scratch_shapes=[pltpu.VMEM((d, INPROJ_TN), jnp.bfloat16)],
        compiler_params=_compiler_params(("arbitrary", "arbitrary")),
        name="in_proj",
    )(h, w)


def _mix_pieces(b_ref, c_ref, u_ref, q_ref, k_ref, v0_ref, v1_ref, g0_ref, g1_ref,
                wconv_ref, tile_cos_ref, tile_sin_ref, row_cos_ref, row_sin_ref,
                row_cos_signed_ref, row_sin_signed_ref,
                mask_ref, qdec_ref, kdec_ref, cdec_ref,
                bc_ref, gr_ref, ucbuf_ref, state_ref, rope_ref):
    rows = MIX_TM
    kscale = RET_QK_DIM ** -0.5

    def rope_piece():
        ca, sa = tile_cos_ref[...], tile_sin_ref[...]
        rope_ref[0] = ca * row_cos_ref[...] - sa * row_sin_ref[...]
        rope_ref[1] = sa * row_cos_signed_ref[...] + ca * row_sin_signed_ref[...]

    def conv_piece(block, chunk):
        cols = slice(block * CONV_TN, (block + 1) * CONV_TN)
        r = slice(chunk * CHUNK, (chunk + 1) * CHUNK)
        lo = SUBLANES + chunk * CHUNK
        uc = c_ref[r, cols].astype(jnp.float32) * u_ref[r, cols].astype(jnp.float32)
        ucbuf_ref[lo:lo + CHUNK, cols] = uc
        prev1 = ucbuf_ref[lo - 1:lo - 1 + CHUNK, cols]
        prev2 = ucbuf_ref[lo - 2:lo - 2 + CHUNK, cols]
        conv = (wconv_ref[0:1, cols] * prev2 + wconv_ref[1:2, cols] * prev1
                + wconv_ref[2:3, cols] * uc)
        bc_ref[r, cols] = (b_ref[r, cols].astype(jnp.float32) * conv).astype(bc_ref.dtype)
        if chunk == rows // CHUNK - 1:
            ucbuf_ref[0:SUBLANES, cols] = uc[CHUNK - SUBLANES:CHUNK, :]

    def retention_piece(chunk, h):
        half = RET_QK_DIM // 2
        r = slice(chunk * CHUNK, (chunk + 1) * CHUNK)
        qk_cols = slice(h * RET_QK_DIM, (h + 1) * RET_QK_DIM)
        v_ref, g_ref = (v0_ref, g0_ref) if h < RET_HEADS // 2 else (v1_ref, g1_ref)
        hv = h % (RET_HEADS // 2)
        v_cols = slice(hv * RET_V_DIM, (hv + 1) * RET_V_DIM)
        t = {}

        def prepare():
            q = q_ref[r, qk_cols].astype(jnp.float32)
            k = k_ref[r, qk_cols].astype(jnp.float32)
            cos2, sin2 = rope_ref[0, r, :], rope_ref[1, r, :]
            q = q * cos2 + pltpu.roll(q, half, axis=1) * sin2
            k = (k * cos2 + pltpu.roll(k, half, axis=1) * sin2) * kscale
            t["qb"] = q.astype(jnp.bfloat16)
            t["kb"] = k.astype(jnp.bfloat16)
            t["qd"] = (q * qdec_ref[h]).astype(jnp.bfloat16)
            t["kd"] = (k * kdec_ref[h]).astype(jnp.bfloat16)

        def first_matmuls():
            v = v_ref[r, v_cols]
            t["scores"] = jax.lax.dot_general(t["qb"], t["kb"], (((1,), (1,)), ((), ())),
                                              preferred_element_type=jnp.float32)
            t["inter"] = jnp.dot(t["qd"], state_ref[h].astype(jnp.bfloat16),
                                 preferred_element_type=jnp.float32)
            t["kv"] = jax.lax.dot_general(t["kd"], v, (((0,), (0,)), ((), ())),
                                          preferred_element_type=jnp.float32)

        def between():
            t["p"] = (t["scores"] * mask_ref[h]).astype(jnp.bfloat16)
            state_ref[h] = cdec_ref[h] * state_ref[h] + t["kv"]

        def second_matmul():
            t["intra"] = jnp.dot(t["p"], v_ref[r, v_cols], preferred_element_type=jnp.float32)

        def finish():
            o = t["intra"] + t["inter"]
            o = o * jax.lax.rsqrt(jnp.mean(o * o, axis=-1, keepdims=True) + EPS)
            g = g_ref[r, v_cols].astype(jnp.float32)
            out_cols = slice(h * RET_V_DIM, (h + 1) * RET_V_DIM)
            gr_ref[r, out_cols] = (_silu(g) * o).astype(gr_ref.dtype)

        return prepare, first_matmuls, between, second_matmul, finish

    retention = [retention_piece(chunk, h)
                 for chunk in range(rows // CHUNK) for h in range(RET_HEADS)]
    conv = [functools.partial(conv_piece, block, chunk)
            for block in range(CONV_WIDTH // CONV_TN) for chunk in range(rows // CHUNK)]
    return rope_piece, conv, retention


def _project_pieces(bc_ref, gr_ref, ac0_ref, ac1_ref, ar0_ref, ar1_ref, x_ref,
                    wco_ref, wro_ref, wo_ref, gain_ref, x1_ref, h2_ref,
                    merged_w_ref, merged_r_ref):
    tn = PROJ_TN
    d = D_MODEL

    def merge_piece(n):
        cols = slice(n * tn, (n + 1) * tn)
        a_c, a_r = (ac0_ref, ar0_ref) if n * tn < COL_BLOCK else (ac1_ref, ar1_ref)
        a_cols = slice((n * tn) % COL_BLOCK, (n * tn) % COL_BLOCK + tn)
        y_conv = jnp.dot(bc_ref[...], wco_ref[:, cols], preferred_element_type=jnp.float32)
        y_ret = jnp.dot(gr_ref[...], wro_ref[:, cols], preferred_element_type=jnp.float32)
        merged = (_sigmoid(a_c[:, a_cols].astype(jnp.float32)) * y_conv
                  + _sigmoid(a_r[:, a_cols].astype(jnp.float32)) * y_ret)
        merged_w_ref[:, cols] = merged.astype(merged_w_ref.dtype)

    def residual_piece(n):
        cols = slice(n * tn, (n + 1) * tn)
        x1_ref[:, cols] = x_ref[:, cols] + jnp.dot(merged_r_ref[...], wo_ref[:, cols],
                                                   preferred_element_type=jnp.float32)

    def norm_piece():
        h2_ref[...] = _rms_norm_f32(x1_ref[...], gain_ref[...]).astype(h2_ref.dtype)

    return ([functools.partial(merge_piece, n) for n in range(d // tn)],
            [functools.partial(residual_piece, n) for n in range(d // tn)],
            norm_piece)


def _interleave(rope, conv, retention, merges, residuals, norm):
    big = merges + residuals
    stride = len(retention) // len(big)
    assert stride * len(big) == len(retention) and len(conv) <= len(retention)
    order = [rope, retention[0][0]]
    for i, (_, first_matmuls, between, second_matmul, finish) in enumerate(retention):
        order.append(first_matmuls)
        if i + 1 < len(retention):
            order.append(retention[i + 1][0])
        if i % stride == 0:
            order.append(big[i // stride])
        if i < len(conv):
            order.append(conv[i])
        order += [between, second_matmul, finish]
    order.append(norm)
    return order


N_MIX_IN = 20
N_PROJ_IN = 9


def _mix_out_kernel(*refs):
    n_in = N_MIX_IN + N_PROJ_IN
    mix_in, proj_in, outs = refs[:N_MIX_IN], refs[N_MIX_IN:n_in], refs[n_in:n_in + 2]
    ucbuf_ref, state_ref, rope_ref = refs[n_in + 2:n_in + 5]
    handoff_a, handoff_b = refs[n_in + 5:n_in + 8], refs[n_in + 8:]
    step = pl.program_id(0)

    @pl.when(step == 0)
    def _():
        ucbuf_ref[0:SUBLANES, :] = jnp.zeros((SUBLANES, CONV_WIDTH), jnp.float32)
        state_ref[...] = jnp.zeros_like(state_ref)
        for ref in handoff_b:
            ref[...] = jnp.zeros_like(ref)

    for parity, (written, read) in enumerate([(handoff_a, handoff_b), (handoff_b, handoff_a)]):
        @pl.when(step % 2 == parity)
        def _(written=written, read=read):
            bc_w, gr_w, merged_w = written
            bc_r, gr_r, merged_r = read
            rope, conv, retention = _mix_pieces(*mix_in, bc_w, gr_w, ucbuf_ref, state_ref,
                                                rope_ref)
            merges, residuals, norm = _project_pieces(bc_r, gr_r, *proj_in, *outs,
                                                      merged_w, merged_r)
            for piece in _interleave(rope, conv, retention, merges, residuals, norm):
                piece()


def _retention_tables(seq):
    half = RET_QK_DIM // 2
    inv = ROPE_BASE ** (-jnp.arange(half, dtype=jnp.float32) / half)
    tile_ang = (jnp.arange(seq // MIX_TM, dtype=jnp.float32) * MIX_TM)[:, None] * inv[None, :]
    row_ang = jnp.arange(MIX_TM, dtype=jnp.float32)[:, None] * inv[None, :]
    both_halves = lambda t: jnp.concatenate([t, t], axis=-1)
    sign = jnp.concatenate([-jnp.ones((half,), jnp.float32), jnp.ones((half,), jnp.float32)])
    row_cos, row_sin = both_halves(jnp.cos(row_ang)), both_halves(jnp.sin(row_ang))
    log_gamma = jnp.log(1.0 - 2.0 ** (-5.0 - jnp.arange(RET_HEADS, dtype=jnp.float32)))
    idx = jnp.arange(CHUNK, dtype=jnp.float32)
    diff = idx[:, None] - idx[None, :]
    mask = jnp.where(diff[None] >= 0,
                     jnp.exp(log_gamma[:, None, None] * jnp.maximum(diff, 0.0)[None]), 0.0)
    q_decay = jnp.exp(log_gamma[:, None] * (idx + 1.0))
    k_decay = jnp.exp(log_gamma[:, None] * (CHUNK - 1.0 - idx))
    chunk_decay = jnp.exp(log_gamma * CHUNK)
    lanes = (RET_HEADS, CHUNK, RET_QK_DIM)
    return dict(
        tile_cos=both_halves(jnp.cos(tile_ang))[:, None, :],
        tile_sin=both_halves(jnp.sin(tile_ang))[:, None, :],
        row_cos=row_cos, row_sin=row_sin,
        row_cos_signed=row_cos * sign, row_sin_signed=row_sin * sign, mask=mask,
        qdec=jnp.broadcast_to(q_decay[:, :, None], lanes),
        kdec=jnp.broadcast_to(k_decay[:, :, None], lanes),
        cdec=jnp.broadcast_to(chunk_decay[:, None, None], (RET_HEADS, 1, RET_V_DIM)))


def _mix_out(proj, x, w_conv, tables, w_co, w_ro, w_o, g_ffn, layer):
    s, d = x.shape
    tm = MIX_TM
    n_tiles = s // tm
    last = n_tiles - 1
    mixed = lambda i: jnp.minimum(i, last)
    merged = lambda i: jnp.clip(i - 1, 0, last)
    finished = lambda i: jnp.maximum(i - 2, 0)
    mix_col = lambda c: pl.BlockSpec((tm, COL_BLOCK), lambda i, c=c: (mixed(i), c))
    tile_rope = pl.BlockSpec((None, 1, RET_QK_DIM), lambda i: (mixed(i), 0, 0))
    row_rope = _constant((tm, RET_QK_DIM))
    proj_col = lambda c: pl.BlockSpec((tm, COL_BLOCK), lambda i, c=c: (merged(i), c))
    proj_rows = pl.BlockSpec((tm, d), lambda i: (finished(i), 0))
    in_specs = [mix_col(COL_B), mix_col(COL_C), mix_col(COL_U), mix_col(COL_Q), mix_col(COL_K),
                mix_col(COL_V0), mix_col(COL_V1), mix_col(COL_G0), mix_col(COL_G1),
                _resident(w_conv.shape, layer),
                tile_rope, tile_rope, row_rope, row_rope, row_rope, row_rope,
                _constant(tables["mask"].shape), _constant(tables["qdec"].shape),
                _constant(tables["kdec"].shape), _constant(tables["cdec"].shape),
                proj_col(COL_ACONV), proj_col(COL_ACONV + 1),
                proj_col(COL_ARET), proj_col(COL_ARET + 1), proj_rows,
                _resident(w_co.shape, layer), _resident(w_ro.shape, layer),
                _resident(w_o.shape, layer), _resident(g_ffn.shape, layer)]
    assert len(in_specs) == N_MIX_IN + N_PROJ_IN
    handoff = [pltpu.VMEM((tm, CONV_WIDTH), jnp.bfloat16),
               pltpu.VMEM((tm, RET_VALUE_WIDTH), jnp.bfloat16),
               pltpu.VMEM((tm, d), jnp.bfloat16)]
    return pl.pallas_call(
        _mix_out_kernel,
        grid=(n_tiles + 2,),
        in_specs=in_specs,
        out_specs=[proj_rows, proj_rows],
        out_shape=[jax.ShapeDtypeStruct((s, d), jnp.float32),
                   jax.ShapeDtypeStruct((s, d), jnp.bfloat16)],
        scratch_shapes=[pltpu.VMEM((SUBLANES + tm, CONV_WIDTH), jnp.float32),
                        pltpu.VMEM((RET_HEADS, RET_QK_DIM, RET_V_DIM), jnp.float32),
                        pltpu.VMEM((2, tm, RET_QK_DIM), jnp.float32)]
                       + handoff + handoff,
        compiler_params=_compiler_params(("arbitrary",)),
        name="mix_out",
    )(*([proj] * 9), w_conv,
      tables["tile_cos"], tables["tile_sin"], tables["row_cos"], tables["row_sin"],
      tables["row_cos_signed"], tables["row_sin_signed"],
      tables["mask"], tables["qdec"], tables["kdec"], tables["cdec"],
      proj, proj, proj, proj, x, w_co, w_ro, w_o, g_ffn)


def _ffn_in_kernel(h2_ref, wg_ref, wu_ref, act_ref, wgbf_ref, wubf_ref):
    @pl.when(pl.program_id(1) == 0)
    def _():
        wgbf_ref[...] = wg_ref[...].astype(wgbf_ref.dtype)
        wubf_ref[...] = wu_ref[...].astype(wubf_ref.dtype)

    tm, tf = act_ref.shape
    for m in range(tm // FFN_IN_SUB_M):
        rows = slice(m * FFN_IN_SUB_M, (m + 1) * FFN_IN_SUB_M)
        h2 = h2_ref[rows, :]
        for n in range(tf // FFN_IN_SUB_N):
            cols = slice(n * FFN_IN_SUB_N, (n + 1) * FFN_IN_SUB_N)
            gate = jnp.dot(h2, wgbf_ref[:, cols], preferred_element_type=jnp.float32)
            up = jnp.dot(h2, wubf_ref[:, cols], preferred_element_type=jnp.float32)
            act_ref[rows, cols] = (_silu(gate) * up).astype(act_ref.dtype)


def _ffn_in(h2, w_in, layer):
    s, d = h2.shape
    tm, tf = FFN_IN_TM, FFN_TF
    nf = D_FF // tf
    return pl.pallas_call(
        _ffn_in_kernel,
        grid=(nf, s // tm),
        in_specs=[pl.BlockSpec((tm, d), lambda f, i: (i, 0)),
                  pl.BlockSpec((None, d, tf), lambda f, i: (layer, 0, f)),
                  pl.BlockSpec((None, d, tf), lambda f, i: (layer, 0, f + nf))],
        out_specs=pl.BlockSpec((tm, tf), lambda f, i: (i, f)),
        out_shape=jax.ShapeDtypeStruct((s, D_FF), jnp.bfloat16),
        scratch_shapes=[pltpu.VMEM((d, tf), jnp.bfloat16),
                        pltpu.VMEM((d, tf), jnp.bfloat16)],
        compiler_params=_compiler_params(("arbitrary", "arbitrary")),
        name="ffn_in",
    )(h2, w_in, w_in)


def _ffn_out_kernel(act_ref, x1_ref, wout_ref, g_ref, *out_refs, final):
    for m in range(act_ref.shape[0] // FFN_OUT_SUB_M):
        rows = slice(m * FFN_OUT_SUB_M, (m + 1) * FFN_OUT_SUB_M)
        x2 = x1_ref[rows, :] + jnp.dot(act_ref[rows, :], wout_ref[...],
                                       preferred_element_type=jnp.float32)
        normed = _rms_norm_f32(x2, g_ref[...])
        if final:
            out_refs[0][rows, :] = normed
        else:
            out_refs[0][rows, :] = x2
            out_refs[1][rows, :] = normed.astype(out_refs[1].dtype)


def _ffn_out(act, x1, w_out, g_next, layer, final):
    s, d = x1.shape
    tm = FFN_OUT_TM
    rows = lambda width: pl.BlockSpec((tm, width), lambda i: (i, 0))
    out_specs = [rows(d)] if final else [rows(d), rows(d)]
    out_shape = [jax.ShapeDtypeStruct((s, d), jnp.float32)]
    if not final:
        out_shape.append(jax.ShapeDtypeStruct((s, d), jnp.bfloat16))
    return pl.pallas_call(
        functools.partial(_ffn_out_kernel, final=final),
        grid=(s // tm,),
        in_specs=[rows(D_FF), rows(d), _resident(w_out.shape, layer),
                  _resident(g_next.shape, layer)],
        out_specs=out_specs,
        out_shape=out_shape,
        compiler_params=_compiler_params(("arbitrary",)),
        name="ffn_out",
    )(act, x1, w_out, g_next)


def kernel(x, g_mix, w_in, w_conv, w_conv_out, w_ret_out, w_out, g_ffn, w_ffn_in,
           w_ffn_out, g_final):
    bsz, seq, d = x.shape
    depth = w_in.shape[0]
    assert bsz == 1 and d == D_MODEL and seq % INPROJ_TM == 0
    bf16 = jnp.bfloat16
    tables = _retention_tables(seq)
    w_co, w_ro, w_o = w_conv_out.astype(bf16), w_ret_out.astype(bf16), w_out.astype(bf16)
    w_fo = w_ffn_out.astype(bf16)
    w_cv = w_conv.reshape(depth, CONV_K, CONV_WIDTH)
    g_next = jnp.concatenate([g_mix[1:], g_final[None]], axis=0).reshape(depth, 1, d)
    g_ffn3 = g_ffn.reshape(depth, 1, d)
    xs = x.reshape(seq, d)
    h = _rms_norm(xs, g_mix[0], bf16)
    for l in range(depth):
        proj = _in_proj(h, w_in, l)
        x1, h2 = _mix_out(proj, xs, w_cv, tables, w_co, w_ro, w_o, g_ffn3, l)
        act = _ffn_in(h2, w_ffn_in, l)
        if l == depth - 1:
            (xs,) = _ffn_out(act, x1, w_fo, g_next, l, True)
        else:
            xs, h = _ffn_out(act, x1, w_fo, g_next, l, False)
    return xs.reshape(bsz, seq, d)
```

```python
import functools

import jax
import jax.numpy as jnp
from jax.experimental import pallas as pl
from jax.experimental.pallas import tpu as pltpu

D_MODEL = 2048
CONV_WIDTH = D_MODEL // 2
CONV_K = 3
RET_HEADS = 8
RET_QK_DIM = D_MODEL // (2 * RET_HEADS)
RET_V_DIM = 2 * RET_QK_DIM
RET_KEY_WIDTH = RET_HEADS * RET_QK_DIM
RET_VALUE_WIDTH = RET_HEADS * RET_V_DIM
CHUNK = 128
ROPE_BASE = 10000.0
D_FF = ((8 * D_MODEL // 3 + 255) // 256) * 256
IN_WIDTH = 3 * CONV_WIDTH + 2 * RET_KEY_WIDTH + 2 * RET_VALUE_WIDTH + 2 * D_MODEL
EPS = 1e-6

COL_BLOCK = 1024
N_COL_BLOCKS = IN_WIDTH // COL_BLOCK
PROJ_ROTATE = 4
PROJ_GROUP = 4 * COL_BLOCK
GROUP_GATES, GROUP_BCUQ, GROUP_KVG = 0, 1, 2
COL_G1 = N_COL_BLOCKS - 1

V7X_VMEM_LIMIT_BYTES = 58 * 1024 * 1024
SUBLANES = 8

NORM_TM = 512
INPROJ_TM = 2048
INPROJ_TN = 1024
MIX_TM = 2 * CHUNK
CONV_TN = 128
PROJ_TN = 2 * D_MODEL // ((MIX_TM // CHUNK) * RET_HEADS)
FFN_IN_TM = 2048
FFN_TF = 512
FFN_IN_SUB_M = 512
FFN_IN_SUB_N = 256
FFN_OUT_TM = 512
FFN_OUT_SUB_M = 256


def _compiler_params(semantics):
    return pltpu.CompilerParams(dimension_semantics=semantics,
                                vmem_limit_bytes=V7X_VMEM_LIMIT_BYTES)


def _rms_norm_f32(x, g):
    var = jnp.mean(x * x, axis=-1, keepdims=True)
    return x * jax.lax.rsqrt(var + EPS) * g


def _sigmoid(x):
    return 0.5 + 0.5 * jnp.tanh(0.5 * x)


def _silu(x):
    h = 0.5 * x
    return h + h * jnp.tanh(h)


def _resident(shape, layer):
    return pl.BlockSpec((None,) + tuple(shape[1:]), lambda *_: (layer, 0, 0),
                        pipeline_mode=pl.Buffered(1))


def _constant(shape):
    return pl.BlockSpec(tuple(shape), lambda *_: (0,) * len(shape),
                        pipeline_mode=pl.Buffered(1))


def _norm_kernel(x_ref, g_ref, o_ref):
    o_ref[...] = _rms_norm_f32(x_ref[...], g_ref[...]).astype(o_ref.dtype)


def _rms_norm(x, g, out_dtype):
    s, d = x.shape
    return pl.pallas_call(
        _norm_kernel,
        grid=(s // NORM_TM,),
        in_specs=[pl.BlockSpec((NORM_TM, d), lambda i: (i, 0)),
                  pl.BlockSpec((1, d), lambda i: (0, 0))],
        out_specs=pl.BlockSpec((NORM_TM, d), lambda i: (i, 0)),
        out_shape=jax.ShapeDtypeStruct((s, d), out_dtype),
        compiler_params=_compiler_params(("arbitrary",)),
        name="rms_norm",
    )(x, g.reshape(1, d))


def _in_proj_kernel(x_ref, w_ref, o_ref, wbf_ref):
    @pl.when(pl.program_id(1) == 0)
    def _():
        wbf_ref[...] = w_ref[...].astype(wbf_ref.dtype)

    o_ref[...] = jnp.dot(x_ref[...], wbf_ref[...],
                         preferred_element_type=jnp.float32).astype(o_ref.dtype)


def _in_proj(h, w, layer):
    s, d = h.shape
    n = w.shape[2]
    assert INPROJ_TN == COL_BLOCK and n == IN_WIDTH
    return pl.pallas_call(
        _in_proj_kernel,
        grid=(n // INPROJ_TN, s // INPROJ_TM),
        in_specs=[pl.BlockSpec((INPROJ_TM, d), lambda j, i: (i, 0)),
                  pl.BlockSpec((None, d, INPROJ_TN), lambda j, i: (layer, 0, j))],
        out_specs=pl.BlockSpec((INPROJ_TM, INPROJ_TN),
                               lambda j, i: (i, (j + PROJ_ROTATE) % N_COL_BLOCKS)),
        out_shape=jax.ShapeDtypeStruct((s, n), jnp.bfloat16),
        scratch_shapes=[pltpu.VMEM((d, INPROJ_TN), jnp.bfloat16)],
        compiler_params=_compiler_params(("arbitrary", "arbitrary")),
        name="in_proj",
    )(h, w)


def _mix_pieces(b_ref, c_ref, u_ref, q_ref, k_ref, v0_ref, v1_ref, g0_ref, g1_ref,
                wconv_ref, tile_cos_ref, tile_sin_ref, row_cos_ref, row_sin_ref,
                row_cos_signed_ref, row_sin_signed_ref,
                mask_ref, qdec_ref, kdec_ref, cdec_ref,
                bc_ref, gr_ref, ucbuf_ref, state_ref, rope_ref):
    rows = MIX_TM
    kscale = RET_QK_DIM ** -0.5

    def rope_piece():
        ca, sa = tile_cos_ref[...], tile_sin_ref[...]
        rope_ref[0] = ca * row_cos_ref[...] - sa * row_sin_ref[...]
        rope_ref[1] = sa * row_cos_signed_ref[...] + ca * row_sin_signed_ref[...]

    def conv_piece(block, chunk):
        cols = slice(block * CONV_TN, (block + 1) * CONV_TN)
        r = slice(chunk * CHUNK, (chunk + 1) * CHUNK)
        lo = SUBLANES + chunk * CHUNK
        uc = c_ref[r, cols].astype(jnp.float32) * u_ref[r, cols].astype(jnp.float32)
        ucbuf_ref[lo:lo + CHUNK, cols] = uc
        prev1 = ucbuf_ref[lo - 1:lo - 1 + CHUNK, cols]
        prev2 = ucbuf_ref[lo - 2:lo - 2 + CHUNK, cols]
        conv = (wconv_ref[0:1, cols] * prev2 + wconv_ref[1:2, cols] * prev1
                + wconv_ref[2:3, cols] * uc)
        bc_ref[r, cols] = (b_ref[r, cols].astype(jnp.float32) * conv).astype(bc_ref.dtype)
        if chunk == rows // CHUNK - 1:
            ucbuf_ref[0:SUBLANES, cols] = uc[CHUNK - SUBLANES:CHUNK, :]

    def retention_piece(chunk, h):
        half = RET_QK_DIM // 2
        r = slice(chunk * CHUNK, (chunk + 1) * CHUNK)
        qk_cols = slice(h * RET_QK_DIM, (h + 1) * RET_QK_DIM)
        v_ref, g_ref = (v0_ref, g0_ref) if h < RET_HEADS // 2 else (v1_ref, g1_ref)
        hv = h % (RET_HEADS // 2)
        v_cols = slice(hv * RET_V_DIM, (hv + 1) * RET_V_DIM)
        t = {}

        def prepare():
            q = q_ref[r, qk_cols].astype(jnp.float32)
            k = k_ref[r, qk_cols].astype(jnp.float32)
            cos2, sin2 = rope_ref[0, r, :], rope_ref[1, r, :]
            q = q * cos2 + pltpu.roll(q, half, axis=1) * sin2
            k = (k * cos2 + pltpu.roll(k, half, axis=1) * sin2) * kscale
            t["qb"] = q.astype(jnp.bfloat16)
            t["kb"] = k.astype(jnp.bfloat16)
            t["qd"] = (q * qdec_ref[h]).astype(jnp.bfloat16)
            t["kd"] = (k * kdec_ref[h]).astype(jnp.bfloat16)

        def first_matmuls():
            v = v_ref[r, v_cols]
            t["scores"] = jax.lax.dot_general(t["qb"], t["kb"], (((1,), (1,)), ((), ())),
                                              preferred_element_type=jnp.float32)
            t["inter"] = jnp.dot(t["qd"], state_ref[h].astype(jnp.bfloat16),
                                 preferred_element_type=jnp.float32)
            t["kv"] = jax.lax.dot_general(t["kd"], v, (((0,), (0,)), ((), ())),
                                          preferred_element_type=jnp.float32)

        def between():
            t["p"] = (t["scores"] * mask_ref[h]).astype(jnp.bfloat16)
            state_ref[h] = cdec_ref[h] * state_ref[h] + t["kv"]

        def second_matmul():
            t["intra"] = jnp.dot(t["p"], v_ref[r, v_cols], preferred_element_type=jnp.float32)

        def finish():
            o = t["intra"] + t["inter"]
            o = o * jax.lax.rsqrt(jnp.mean(o * o, axis=-1, keepdims=True) + EPS)
            g = g_ref[r, v_cols].astype(jnp.float32)
            out_cols = slice(h * RET_V_DIM, (h + 1) * RET_V_DIM)
            gr_ref[r, out_cols] = (_silu(g) * o).astype(gr_ref.dtype)

        return prepare, first_matmuls, between, second_matmul, finish

    retention = [retention_piece(chunk, h)
                 for chunk in range(rows // CHUNK) for h in range(RET_HEADS)]
    conv = [functools.partial(conv_piece, block, chunk)
            for block in range(CONV_WIDTH // CONV_TN) for chunk in range(rows // CHUNK)]
    return rope_piece, conv, retention


def _project_pieces(bc_ref, gr_ref, a_conv_ref, a_ret_ref, x_ref,
                    wco_ref, wro_ref, wo_ref, gain_ref, x1_ref, h2_ref,
                    merged_w_ref, merged_r_ref):
    tn = PROJ_TN
    d = D_MODEL

    def merge_piece(n):
        cols = slice(n * tn, (n + 1) * tn)
        y_conv = jnp.dot(bc_ref[...], wco_ref[:, cols], preferred_element_type=jnp.float32)
        y_ret = jnp.dot(gr_ref[...], wro_ref[:, cols], preferred_element_type=jnp.float32)
        merged = (_sigmoid(a_conv_ref[:, cols].astype(jnp.float32)) * y_conv
                  + _sigmoid(a_ret_ref[:, cols].astype(jnp.float32)) * y_ret)
        merged_w_ref[:, cols] = merged.astype(merged_w_ref.dtype)

    def residual_piece(n):
        cols = slice(n * tn, (n + 1) * tn)
        x1_ref[:, cols] = x_ref[:, cols] + jnp.dot(merged_r_ref[...], wo_ref[:, cols],
                                                   preferred_element_type=jnp.float32)

    def norm_piece():
        h2_ref[...] = _rms_norm_f32(x1_ref[...], gain_ref[...]).astype(h2_ref.dtype)

    return ([functools.partial(merge_piece, n) for n in range(d // tn)],
            [functools.partial(residual_piece, n) for n in range(d // tn)],
            norm_piece)


def _interleave(rope, conv, retention, merges, residuals, norm):
    big = merges + residuals
    stride = len(retention) // len(big)
    assert stride * len(big) == len(retention) and len(conv) <= len(retention)
    order = [rope, retention[0][0]]
    for i, (_, first_matmuls, between, second_matmul, finish) in enumerate(retention):
        order.append(first_matmuls)
        if i + 1 < len(retention):
            order.append(retention[i + 1][0])
        if i % stride == 0:
            order.append(big[i // stride])
        if i < len(conv):
            order.append(conv[i])
        order += [between, second_matmul, finish]
    order.append(norm)
    return order


N_MIX_IN = 14
N_PROJ_IN = 6


def _mix_out_kernel(*refs):
    n_in = N_MIX_IN + N_PROJ_IN
    (bcuq_ref, kvg_ref, g1_ref), tables = refs[:3], refs[3:N_MIX_IN]
    gates_ref, proj_rest = refs[N_MIX_IN], refs[N_MIX_IN + 1:n_in]
    outs = refs[n_in:n_in + 2]
    block = lambda ref, first, n=1: ref.at[:, first * COL_BLOCK:(first + n) * COL_BLOCK]
    mix_in = ([block(bcuq_ref, j) for j in range(4)] + [block(kvg_ref, j) for j in range(4)]
              + [g1_ref] + list(tables))
    proj_in = [block(gates_ref, 0, 2), block(gates_ref, 2, 2)] + list(proj_rest)
    ucbuf_ref, state_ref, rope_ref = refs[n_in + 2:n_in + 5]
    handoff_a, handoff_b = refs[n_in + 5:n_in + 8], refs[n_in + 8:]
    step = pl.program_id(0)

    @pl.when(step == 0)
    def _():
        ucbuf_ref[0:SUBLANES, :] = jnp.zeros((SUBLANES, CONV_WIDTH), jnp.float32)
        state_ref[...] = jnp.zeros_like(state_ref)
        for ref in handoff_b:
            ref[...] = jnp.zeros_like(ref)

    for parity, (written, read) in enumerate([(handoff_a, handoff_b), (handoff_b, handoff_a)]):
        @pl.when(step % 2 == parity)
        def _(written=written, read=read):
            bc_w, gr_w, merged_w = written
            bc_r, gr_r, merged_r = read
            rope, conv, retention = _mix_pieces(*mix_in, bc_w, gr_w, ucbuf_ref, state_ref,
                                                rope_ref)
            merges, residuals, norm = _project_pieces(bc_r, gr_r, *proj_in, *outs,
                                                      merged_w, merged_r)
            for piece in _interleave(rope, conv, retention, merges, residuals, norm):
                piece()


def _retention_tables(seq):
    half = RET_QK_DIM // 2
    inv = ROPE_BASE ** (-jnp.arange(half, dtype=jnp.float32) / half)
    tile_ang = (jnp.arange(seq // MIX_TM, dtype=jnp.float32) * MIX_TM)[:, None] * inv[None, :]
    row_ang = jnp.arange(MIX_TM, dtype=jnp.float32)[:, None] * inv[None, :]
    both_halves = lambda t: jnp.concatenate([t, t], axis=-1)
    sign = jnp.concatenate([-jnp.ones((half,), jnp.float32), jnp.ones((half,), jnp.float32)])
    row_cos, row_sin = both_halves(jnp.cos(row_ang)), both_halves(jnp.sin(row_ang))
    log_gamma = jnp.log(1.0 - 2.0 ** (-5.0 - jnp.arange(RET_HEADS, dtype=jnp.float32)))
    idx = jnp.arange(CHUNK, dtype=jnp.float32)
    diff = idx[:, None] - idx[None, :]
    mask = jnp.where(diff[None] >= 0,
                     jnp.exp(log_gamma[:, None, None] * jnp.maximum(diff, 0.0)[None]), 0.0)
    q_decay = jnp.exp(log_gamma[:, None] * (idx + 1.0))
    k_decay = jnp.exp(log_gamma[:, None] * (CHUNK - 1.0 - idx))
    chunk_decay = jnp.exp(log_gamma * CHUNK)
    lanes = (RET_HEADS, CHUNK, RET_QK_DIM)
    return dict(
        tile_cos=both_halves(jnp.cos(tile_ang))[:, None, :],
        tile_sin=both_halves(jnp.sin(tile_ang))[:, None, :],
        row_cos=row_cos, row_sin=row_sin,
        row_cos_signed=row_cos * sign, row_sin_signed=row_sin * sign, mask=mask,
        qdec=jnp.broadcast_to(q_decay[:, :, None], lanes),
        kdec=jnp.broadcast_to(k_decay[:, :, None], lanes),
        cdec=jnp.broadcast_to(chunk_decay[:, None, None], (RET_HEADS, 1, RET_V_DIM)))


def _mix_out(proj, x, w_conv, tables, w_co, w_ro, w_o, g_ffn, layer):
    s, d = x.shape
    tm = MIX_TM
    n_tiles = s // tm
    last = n_tiles - 1
    mixed = lambda i: jnp.minimum(i, last)
    merged = lambda i: jnp.clip(i - 1, 0, last)
    finished = lambda i: jnp.maximum(i - 2, 0)
    mix_group = lambda g: pl.BlockSpec((tm, PROJ_GROUP), lambda i, g=g: (mixed(i), g))
    tile_rope = pl.BlockSpec((None, 1, RET_QK_DIM), lambda i: (mixed(i), 0, 0))
    row_rope = _constant((tm, RET_QK_DIM))
    proj_rows = pl.BlockSpec((tm, d), lambda i: (finished(i), 0))
    in_specs = [mix_group(GROUP_BCUQ), mix_group(GROUP_KVG),
                pl.BlockSpec((tm, COL_BLOCK), lambda i: (mixed(i), COL_G1)),
                _resident(w_conv.shape, layer),
                tile_rope, tile_rope, row_rope, row_rope, row_rope, row_rope,
                _constant(tables["mask"].shape), _constant(tables["qdec"].shape),
                _constant(tables["kdec"].shape), _constant(tables["cdec"].shape),
                pl.BlockSpec((tm, PROJ_GROUP), lambda i: (merged(i), GROUP_GATES)), proj_rows,
                _resident(w_co.shape, layer), _resident(w_ro.shape, layer),
                _resident(w_o.shape, layer), _resident(g_ffn.shape, layer)]
    assert len(in_specs) == N_MIX_IN + N_PROJ_IN
    handoff = [pltpu.VMEM((tm, CONV_WIDTH), jnp.bfloat16),
               pltpu.VMEM((tm, RET_VALUE_WIDTH), jnp.bfloat16),
               pltpu.VMEM((tm, d), jnp.bfloat16)]
    return pl.pallas_call(
        _mix_out_kernel,
        grid=(n_tiles + 2,),
        in_specs=in_specs,
        out_specs=[proj_rows, proj_rows],
        out_shape=[jax.ShapeDtypeStruct((s, d), jnp.float32),
                   jax.ShapeDtypeStruct((s, d), jnp.bfloat16)],
        scratch_shapes=[pltpu.VMEM((SUBLANES + tm, CONV_WIDTH), jnp.float32),
                        pltpu.VMEM((RET_HEADS, RET_QK_DIM, RET_V_DIM), jnp.float32),
                        pltpu.VMEM((2, tm, RET_QK_DIM), jnp.float32)]
                       + handoff + handoff,
        compiler_params=_compiler_params(("arbitrary",)),
        name="mix_out",
    )(proj, proj, proj, w_conv,
      tables["tile_cos"], tables["tile_sin"], tables["row_cos"], tables["row_sin"],
      tables["row_cos_signed"], tables["row_sin_signed"],
      tables["mask"], tables["qdec"], tables["kdec"], tables["cdec"],
      proj, x, w_co, w_ro, w_o, g_ffn)


def _ffn_in_kernel(h2_ref, wg_ref, wu_ref, act_ref, wgbf_ref, wubf_ref):
    @pl.when(pl.program_id(1) == 0)
    def _():
        wgbf_ref[...] = wg_ref[...].astype(wgbf_ref.dtype)
        wubf_ref[...] = wu_ref[...].astype(wubf_ref.dtype)

    tm, tf = act_ref.shape
    for m in range(tm // FFN_IN_SUB_M):
        rows = slice(m * FFN_IN_SUB_M, (m + 1) * FFN_IN_SUB_M)
        h2 = h2_ref[rows, :]
        for n in range(tf // FFN_IN_SUB_N):
            cols = slice(n * FFN_IN_SUB_N, (n + 1) * FFN_IN_SUB_N)
            gate = jnp.dot(h2, wgbf_ref[:, cols], preferred_element_type=jnp.float32)
            up = jnp.dot(h2, wubf_ref[:, cols], preferred_element_type=jnp.float32)
            act_ref[rows, cols] = (_silu(gate) * up).astype(act_ref.dtype)


def _ffn_in(h2, w_in, layer):
    s, d = h2.shape
    tm, tf = FFN_IN_TM, FFN_TF
    nf = D_FF // tf
    return pl.pallas_call(
        _ffn_in_kernel,
        grid=(nf, s // tm),
        in_specs=[pl.BlockSpec((tm, d), lambda f, i: (i, 0)),
                  pl.BlockSpec((None, d, tf), lambda f, i: (layer, 0, f)),
                  pl.BlockSpec((None, d, tf), lambda f, i: (layer, 0, f + nf))],
        out_specs=pl.BlockSpec((tm, tf), lambda f, i: (i, f)),
        out_shape=jax.ShapeDtypeStruct((s, D_FF), jnp.bfloat16),
        scratch_shapes=[pltpu.VMEM((d, tf), jnp.bfloat16),
                        pltpu.VMEM((d, tf), jnp.bfloat16)],
        compiler_params=_compiler_params(("arbitrary", "arbitrary")),
        name="ffn_in",
    )(h2, w_in, w_in)


def _ffn_out_kernel(act_ref, x1_ref, wout_ref, g_ref, *out_refs, final):
    for m in range(act_ref.shape[0] // FFN_OUT_SUB_M):
        rows = slice(m * FFN_OUT_SUB_M, (m + 1) * FFN_OUT_SUB_M)
        x2 = x1_ref[rows, :] + jnp.dot(act_ref[rows, :], wout_ref[...],
                                       preferred_element_type=jnp.float32)
        normed = _rms_norm_f32(x2, g_ref[...])
        if final:
            out_refs[0][rows, :] = normed
        else:
            out_refs[0][rows, :] = x2
            out_refs[1][rows, :] = normed.astype(out_refs[1].dtype)


def _ffn_out(act, x1, w_out, g_next, layer, final):
    s, d = x1.shape
    tm = FFN_OUT_TM
    rows = lambda width: pl.BlockSpec((tm, width), lambda i: (i, 0))
    out_specs = [rows(d)] if final else [rows(d), rows(d)]
    out_shape = [jax.ShapeDtypeStruct((s, d), jnp.float32)]
    if not final:
        out_shape.append(jax.ShapeDtypeStruct((s, d), jnp.bfloat16))
    return pl.pallas_call(
        functools.partial(_ffn_out_kernel, final=final),
        grid=(s // tm,),
        in_specs=[rows(D_FF), rows(d), _resident(w_out.shape, layer),
                  _resident(g_next.shape, layer)],
        out_specs=out_specs,
        out_shape=out_shape,
        compiler_params=_compiler_params(("arbitrary",)),
        name="ffn_out",
    )(act, x1, w_out, g_next)


def kernel(x, g_mix, w_in, w_conv, w_conv_out, w_ret_out, w_out, g_ffn, w_ffn_in,
           w_ffn_out, g_final):
    bsz, seq, d = x.shape
    depth = w_in.shape[0]
    assert bsz == 1 and d == D_MODEL and seq % INPROJ_TM == 0
    bf16 = jnp.bfloat16
    tables = _retention_tables(seq)
    w_co, w_ro, w_o = w_conv_out.astype(bf16), w_ret_out.astype(bf16), w_out.astype(bf16)
    w_fo = w_ffn_out.astype(bf16)
    w_cv = w_conv.reshape(depth, CONV_K, CONV_WIDTH)
    g_next = jnp.concatenate([g_mix[1:], g_final[None]], axis=0).reshape(depth, 1, d)
    g_ffn3 = g_ffn.reshape(depth, 1, d)
    xs = x.reshape(seq, d)
    h = _rms_norm(xs, g_mix[0], bf16)
    for l in range(depth):
        proj = _in_proj(h, w_in, l)
        x1, h2 = _mix_out(proj, xs, w_cv, tables, w_co, w_ro, w_o, g_ffn3, l)
        act = _ffn_in(h2, w_ffn_in, l)
        if l == depth - 1:
            (xs,) = _ffn_out(act, x1, w_fo, g_next, l, True)
        else:
            xs, h = _ffn_out(act, x1, w_fo, g_next, l, False)
    return xs.reshape(bsz, seq, d)
```

```python
import functools

import jax
import jax.numpy as jnp
from jax.experimental import pallas as pl
from jax.experimental.pallas import tpu as pltpu

D_MODEL = 2048
CONV_WIDTH = D_MODEL // 2
CONV_K = 3
RET_HEADS = 8
RET_QK_DIM = D_MODEL // (2 * RET_HEADS)
RET_V_DIM = 2 * RET_QK_DIM
RET_KEY_WIDTH = RET_HEADS * RET_QK_DIM
RET_VALUE_WIDTH = RET_HEADS * RET_V_DIM
CHUNK = 128
ROPE_BASE = 10000.0
D_FF = ((8 * D_MODEL // 3 + 255) // 256) * 256
IN_WIDTH = 3 * CONV_WIDTH + 2 * RET_KEY_WIDTH + 2 * RET_VALUE_WIDTH + 2 * D_MODEL
EPS = 1e-6

COL_BLOCK = 1024
COL_B, COL_C, COL_U, COL_Q, COL_K = 0, 1, 2, 3, 4
COL_V0, COL_V1, COL_G0, COL_G1 = 5, 6, 7, 8
COL_ACONV = 9
COL_ARET = 11

V7X_VMEM_LIMIT_BYTES = 58 * 1024 * 1024
SUBLANES = 8

NORM_TM = 512
INPROJ_TM = 2048
INPROJ_TN = 1024
MIX_TM = 2 * CHUNK
CONV_TN = 128
PROJ_TN = 2 * D_MODEL // ((MIX_TM // CHUNK) * RET_HEADS)
FFN_IN_TM = 2048
FFN_TF = 512
FFN_IN_SUB_M = 512
FFN_IN_SUB_N = 256
FFN_OUT_TM = 512
FFN_OUT_SUB_M = 256


def _compiler_params(semantics):
    return pltpu.CompilerParams(dimension_semantics=semantics,
                                vmem_limit_bytes=V7X_VMEM_LIMIT_BYTES)


def _rms_norm_f32(x, g):
    var = jnp.mean(x * x, axis=-1, keepdims=True)
    return x * jax.lax.rsqrt(var + EPS) * g


def _sigmoid(x):
    return 0.5 + 0.5 * jnp.tanh(0.5 * x)


def _silu(x):
    h = 0.5 * x
    return h + h * jnp.tanh(h)


def _resident(shape, layer):
    return pl.BlockSpec((None,) + tuple(shape[1:]), lambda *_: (layer, 0, 0),
                        pipeline_mode=pl.Buffered(1))


def _constant(shape):
    return pl.BlockSpec(tuple(shape), lambda *_: (0,) * len(shape),
                        pipeline_mode=pl.Buffered(1))


def _norm_kernel(x_ref, g_ref, o_ref):
    o_ref[...] = _rms_norm_f32(x_ref[...], g_ref[...]).astype(o_ref.dtype)


def _rms_norm(x, g, out_dtype):
    s, d = x.shape
    return pl.pallas_call(
        _norm_kernel,
        grid=(s // NORM_TM,),
        in_specs=[pl.BlockSpec((NORM_TM, d), lambda i: (i, 0)),
                  pl.BlockSpec((1, d), lambda i: (0, 0))],
        out_specs=pl.BlockSpec((NORM_TM, d), lambda i: (i, 0)),
        out_shape=jax.ShapeDtypeStruct((s, d), out_dtype),
        compiler_params=_compiler_params(("arbitrary",)),
        name="rms_norm",
    )(x, g.reshape(1, d))


def _in_proj_kernel(x_ref, w_ref, o_ref, wbf_ref):
    @pl.when(pl.program_id(1) == 0)
    def _():
        wbf_ref[...] = w_ref[...].astype(wbf_ref.dtype)

    o_ref[...] = jnp.dot(x_ref[...], wbf_ref[...],
                         preferred_element_type=jnp.float32).astype(o_ref.dtype)


def _in_proj(h, w, layer):
    s, d = h.shape
    n = w.shape[2]
    return pl.pallas_call(
        _in_proj_kernel,
        grid=(n // INPROJ_TN, s // INPROJ_TM),
        in_specs=[pl.BlockSpec((INPROJ_TM, d), lambda j, i: (i, 0)),
                  pl.BlockSpec((None, d, INPROJ_TN), lambda j, i: (layer, 0, j))],
        out_specs=pl.BlockSpec((INPROJ_TM, INPROJ_TN), lambda j, i: (i, j)),
        out_shape=jax.ShapeDtypeStruct((s, n), jnp.bfloat16),
        scratch_shapes=[pltpu.VMEM((d, INPROJ_TN), jnp.bfloat16)],
        compiler_params=_compiler_params(("arbitrary", "arbitrary")),
        name="in_proj",
    )(h, w)


def _mix_pieces(b_ref, c_ref, u_ref, q_ref, k_ref, v0_ref, v1_ref, g0_ref, g1_ref,
                wconv_ref, tile_cos_ref, tile_sin_ref, row_cos_ref, row_sin_ref,
                row_cos_signed_ref, row_sin_signed_ref,
                mask_ref, qdec_ref, kdec_ref, cdec_ref,
                bc_ref, gr_ref, ucbuf_ref, state_ref, rope_ref):
    rows = MIX_TM
    kscale = RET_QK_DIM ** -0.5

    def rope_piece():
        ca, sa = tile_cos_ref[...], tile_sin_ref[...]
        rope_ref[0] = ca * row_cos_ref[...] - sa * row_sin_ref[...]
        rope_ref[1] = sa * row_cos_signed_ref[...] + ca * row_sin_signed_ref[...]

    def conv_piece(block, chunk):
        cols = slice(block * CONV_TN, (block + 1) * CONV_TN)
        r = slice(chunk * CHUNK, (chunk + 1) * CHUNK)
        lo = SUBLANES + chunk * CHUNK
        uc = c_ref[r, cols].astype(jnp.float32) * u_ref[r, cols].astype(jnp.float32)
        ucbuf_ref[lo:lo + CHUNK, cols] = uc
        prev1 = ucbuf_ref[lo - 1:lo - 1 + CHUNK, cols]
        prev2 = ucbuf_ref[lo - 2:lo - 2 + CHUNK, cols]
        conv = (wconv_ref[0:1, cols] * prev2 + wconv_ref[1:2, cols] * prev1
                + wconv_ref[2:3, cols] * uc)
        bc_ref[r, cols] = (b_ref[r, cols].astype(jnp.float32) * conv).astype(bc_ref.dtype)
        if chunk == rows // CHUNK - 1:
            ucbuf_ref[0:SUBLANES, cols] = uc[CHUNK - SUBLANES:CHUNK, :]

    def retention_piece(chunk, h):
        half = RET_QK_DIM // 2
        r = slice(chunk * CHUNK, (chunk + 1) * CHUNK)
        qk_cols = slice(h * RET_QK_DIM, (h + 1) * RET_QK_DIM)
        v_ref, g_ref = (v0_ref, g0_ref) if h < RET_HEADS // 2 else (v1_ref, g1_ref)
        hv = h % (RET_HEADS // 2)
        v_cols = slice(hv * RET_V_DIM, (hv + 1) * RET_V_DIM)
        t = {}

        def prepare():
            q = q_ref[r, qk_cols].astype(jnp.float32)
            k = k_ref[r, qk_cols].astype(jnp.float32)
            cos2, sin2 = rope_ref[0, r, :], rope_ref[1, r, :]
            q = q * cos2 + pltpu.roll(q, half, axis=1) * sin2
            k = (k * cos2 + pltpu.roll(k, half, axis=1) * sin2) * kscale
            t["qb"] = q.astype(jnp.bfloat16)
            t["kb"] = k.astype(jnp.bfloat16)
            t["qd"] = (q * qdec_ref[h]).astype(jnp.bfloat16)
            t["kd"] = (k * kdec_ref[h]).astype(jnp.bfloat16)

        def first_matmuls():
            v = v_ref[r, v_cols]
            t["scores"] = jax.lax.dot_general(t["qb"], t["kb"], (((1,), (1,)), ((), ())),
                                              preferred_element_type=jnp.float32)
            t["kv"] = jax.lax.dot_general(t["kd"], v, (((0,), (0,)), ((), ())),
                                          preferred_element_type=jnp.float32)

        def between():
            t["p"] = (t["scores"] * mask_ref[h]).astype(jnp.bfloat16)
            state = state_ref[h]
            t["state"] = state.astype(jnp.bfloat16)
            state_ref[h] = cdec_ref[h] * state + t["kv"]

        def second_matmul():
            lhs = jnp.concatenate([t["p"], t["qd"]], axis=1)
            rhs = jnp.concatenate([v_ref[r, v_cols], t["state"]], axis=0)
            t["o"] = jnp.dot(lhs, rhs, preferred_element_type=jnp.float32)

        def finish():
            o = t["o"]
            o = o * jax.lax.rsqrt(jnp.mean(o * o, axis=-1, keepdims=True) + EPS)
            g = g_ref[r, v_cols].astype(jnp.float32)
            out_cols = slice(h * RET_V_DIM, (h + 1) * RET_V_DIM)
            gr_ref[r, out_cols] = (_silu(g) * o).astype(gr_ref.dtype)

        return prepare, first_matmuls, between, second_matmul, finish

    retention = [retention_piece(chunk, h)
                 for chunk in range(rows // CHUNK) for h in range(RET_HEADS)]
    conv = [functools.partial(conv_piece, block, chunk)
            for block in range(CONV_WIDTH // CONV_TN) for chunk in range(rows // CHUNK)]
    return rope_piece, conv, retention


def _project_pieces(bc_ref, gr_ref, ac0_ref, ac1_ref, ar0_ref, ar1_ref, x_ref,
                    wco_ref, wro_ref, wo_ref, gain_ref, x1_ref, h2_ref,
                    merged_w_ref, merged_r_ref):
    tn = PROJ_TN
    d = D_MODEL

    def merge_piece(n):
        cols = slice(n * tn, (n + 1) * tn)
        a_c, a_r = (ac0_ref, ar0_ref) if n * tn < COL_BLOCK else (ac1_ref, ar1_ref)
        a_cols = slice((n * tn) % COL_BLOCK, (n * tn) % COL_BLOCK + tn)
        y_conv = jnp.dot(bc_ref[...], wco_ref[:, cols], preferred_element_type=jnp.float32)
        y_ret = jnp.dot(gr_ref[...], wro_ref[:, cols], preferred_element_type=jnp.float32)
        merged = (_sigmoid(a_c[:, a_cols].astype(jnp.float32)) * y_conv
                  + _sigmoid(a_r[:, a_cols].astype(jnp.float32)) * y_ret)
        merged_w_ref[:, cols] = merged.astype(merged_w_ref.dtype)

    def residual_piece(n):
        cols = slice(n * tn, (n + 1) * tn)
        x1_ref[:, cols] = x_ref[:, cols] + jnp.dot(merged_r_ref[...], wo_ref[:, cols],
                                                   preferred_element_type=jnp.float32)

    def norm_piece():
        h2_ref[...] = _rms_norm_f32(x1_ref[...], gain_ref[...]).astype(h2_ref.dtype)

    return ([functools.partial(merge_piece, n) for n in range(d // tn)],
            [functools.partial(residual_piece, n) for n in range(d // tn)],
            norm_piece)


def _interleave(rope, conv, retention, merges, residuals, norm):
    big = merges + residuals
    stride = len(retention) // len(big)
    assert stride * len(big) == len(retention) and len(conv) <= len(retention)
    order = [rope, retention[0][0]]
    for i, (_, first_matmuls, between, second_matmul, finish) in enumerate(retention):
        order.append(first_matmuls)
        if i + 1 < len(retention):
            order.append(retention[i + 1][0])
        if i % stride == 0:
            order.append(big[i // stride])
        if i < len(conv):
            order.append(conv[i])
        order += [between, second_matmul, finish]
    order.append(norm)
    return order


N_MIX_IN = 20
N_PROJ_IN = 9


def _mix_out_kernel(*refs):
    n_in = N_MIX_IN + N_PROJ_IN
    mix_in, proj_in, outs = refs[:N_MIX_IN], refs[N_MIX_IN:n_in], refs[n_in:n_in + 2]
    ucbuf_ref, state_ref, rope_ref = refs[n_in + 2:n_in + 5]
    handoff_a, handoff_b = refs[n_in + 5:n_in + 8], refs[n_in + 8:]
    step = pl.program_id(0)

    @pl.when(step == 0)
    def _():
        ucbuf_ref[0:SUBLANES, :] = jnp.zeros((SUBLANES, CONV_WIDTH), jnp.float32)
        state_ref[...] = jnp.zeros_like(state_ref)
        for ref in handoff_b:
            ref[...] = jnp.zeros_like(ref)

    for parity, (written, read) in enumerate([(handoff_a, handoff_b), (handoff_b, handoff_a)]):
        @pl.when(step % 2 == parity)
        def _(written=written, read=read):
            bc_w, gr_w, merged_w = written
            bc_r, gr_r, merged_r = read
            rope, conv, retention = _mix_pieces(*mix_in, bc_w, gr_w, ucbuf_ref, state_ref,
                                                rope_ref)
            merges, residuals, norm = _project_pieces(bc_r, gr_r, *proj_in, *outs,
                                                      merged_w, merged_r)
            for piece in _interleave(rope, conv, retention, merges, residuals, norm):
                piece()


def _retention_tables(seq):
    half = RET_QK_DIM // 2
    inv = ROPE_BASE ** (-jnp.arange(half, dtype=jnp.float32) / half)
    tile_ang = (jnp.arange(seq // MIX_TM, dtype=jnp.float32) * MIX_TM)[:, None] * inv[None, :]
    row_ang = jnp.arange(MIX_TM, dtype=jnp.float32)[:, None] * inv[None, :]
    both_halves = lambda t: jnp.concatenate([t, t], axis=-1)
    sign = jnp.concatenate([-jnp.ones((half,), jnp.float32), jnp.ones((half,), jnp.float32)])
    row_cos, row_sin = both_halves(jnp.cos(row_ang)), both_halves(jnp.sin(row_ang))
    log_gamma = jnp.log(1.0 - 2.0 ** (-5.0 - jnp.arange(RET_HEADS, dtype=jnp.float32)))
    idx = jnp.arange(CHUNK, dtype=jnp.float32)
    diff = idx[:, None] - idx[None, :]
    mask = jnp.where(diff[None] >= 0,
                     jnp.exp(log_gamma[:, None, None] * jnp.maximum(diff, 0.0)[None]), 0.0)
    q_decay = jnp.exp(log_gamma[:, None] * (idx + 1.0))
    k_decay = jnp.exp(log_gamma[:, None] * (CHUNK - 1.0 - idx))
    chunk_decay = jnp.exp(log_gamma * CHUNK)
    lanes = (RET_HEADS, CHUNK, RET_QK_DIM)
    return dict(
        tile_cos=both_halves(jnp.cos(tile_ang))[:, None, :],
        tile_sin=both_halves(jnp.sin(tile_ang))[:, None, :],
        row_cos=row_cos, row_sin=row_sin,
        row_cos_signed=row_cos * sign, row_sin_signed=row_sin * sign, mask=mask,
        qdec=jnp.broadcast_to(q_decay[:, :, None], lanes),
        kdec=jnp.broadcast_to(k_decay[:, :, None], lanes),
        cdec=jnp.broadcast_to(chunk_decay[:, None, None], (RET_HEADS, 1, RET_V_DIM)))


def _mix_out(proj, x, w_conv, tables, w_co, w_ro, w_o, g_ffn, layer):
    s, d = x.shape
    tm = MIX_TM
    n_tiles = s // tm
    last = n_tiles - 1
    mixed = lambda i: jnp.minimum(i, last)
    merged = lambda i: jnp.clip(i - 1, 0, last)
    finished = lambda i: jnp.maximum(i - 2, 0)
    mix_col = lambda c: pl.BlockSpec((tm, COL_BLOCK), lambda i, c=c: (mixed(i), c))
    tile_rope = pl.BlockSpec((None, 1, RET_QK_DIM), lambda i: (mixed(i), 0, 0))
    row_rope = _constant((tm, RET_QK_DIM))
    proj_col = lambda c: pl.BlockSpec((tm, COL_BLOCK), lambda i, c=c: (merged(i), c))
    proj_rows = pl.BlockSpec((tm, d), lambda i: (finished(i), 0))
    in_specs = [mix_col(COL_B), mix_col(COL_C), mix_col(COL_U), mix_col(COL_Q), mix_col(COL_K),
                mix_col(COL_V0), mix_col(COL_V1), mix_col(COL_G0), mix_col(COL_G1),
                _resident(w_conv.shape, layer),
                tile_rope, tile_rope, row_rope, row_rope, row_rope, row_rope,
                _constant(tables["mask"].shape), _constant(tables["qdec"].shape),
                _constant(tables["kdec"].shape), _constant(tables["cdec"].shape),
                proj_col(COL_ACONV), proj_col(COL_ACONV + 1),
                proj_col(COL_ARET), proj_col(COL_ARET + 1), proj_rows,
                _resident(w_co.shape, layer), _resident(w_ro.shape, layer),
                _resident(w_o.shape, layer), _resident(g_ffn.shape, layer)]
    assert len(in_specs) == N_MIX_IN + N_PROJ_IN
    handoff = [pltpu.VMEM((tm, CONV_WIDTH), jnp.bfloat16),
               pltpu.VMEM((tm, RET_VALUE_WIDTH), jnp.bfloat16),
               pltpu.VMEM((tm, d), jnp.bfloat16)]
    return pl.pallas_call(
        _mix_out_kernel,
        grid=(n_tiles + 2,),
        in_specs=in_specs,
        out_specs=[proj_rows, proj_rows],
        out_shape=[jax.ShapeDtypeStruct((s, d), jnp.float32),
                   jax.ShapeDtypeStruct((s, d), jnp.bfloat16)],
        scratch_shapes=[pltpu.VMEM((SUBLANES + tm, CONV_WIDTH), jnp.float32),
                        pltpu.VMEM((RET_HEADS, RET_QK_DIM, RET_V_DIM), jnp.float32),
                        pltpu.VMEM((2, tm, RET_QK_DIM), jnp.float32)]
                       + handoff + handoff,
        compiler_params=_compiler_params(("arbitrary",)),
        name="mix_out",
    )(*([proj] * 9), w_conv,
      tables["tile_cos"], tables["tile_sin"], tables["row_cos"], tables["row_sin"],
      tables["row_cos_signed"], tables["row_sin_signed"],
      tables["mask"], tables["qdec"], tables["kdec"], tables["cdec"],
      proj, proj, proj, proj, x, w_co, w_ro, w_o, g_ffn)


def _ffn_in_kernel(h2_ref, wg_ref, wu_ref, act_ref, wgbf_ref, wubf_ref):
    @pl.when(pl.program_id(1) == 0)
    def _():
        wgbf_ref[...] = wg_ref[...].astype(wgbf_ref.dtype)
        wubf_ref[...] = wu_ref[...].astype(wubf_ref.dtype)

    tm, tf = act_ref.shape
    for m in range(tm // FFN_IN_SUB_M):
        rows = slice(m * FFN_IN_SUB_M, (m + 1) * FFN_IN_SUB_M)
        h2 = h2_ref[rows, :]
        for n in range(tf // FFN_IN_SUB_N):
            cols = slice(n * FFN_IN_SUB_N, (n + 1) * FFN_IN_SUB_N)
            gate = jnp.dot(h2, wgbf_ref[:, cols], preferred_element_type=jnp.float32)
            up = jnp.dot(h2, wubf_ref[:, cols], preferred_element_type=jnp.float32)
            act_ref[rows, cols] = (_silu(gate) * up).astype(act_ref.dtype)


def _ffn_in(h2, w_in, layer):
    s, d = h2.shape
    tm, tf = FFN_IN_TM, FFN_TF
    nf = D_FF // tf
    return pl.pallas_call(
        _ffn_in_kernel,
        grid=(nf, s // tm),
        in_specs=[pl.BlockSpec((tm, d), lambda f, i: (i, 0)),
                  pl.BlockSpec((None, d, tf), lambda f, i: (layer, 0, f)),
                  pl.BlockSpec((None, d, tf), lambda f, i: (layer, 0, f + nf))],
        out_specs=pl.BlockSpec((tm, tf), lambda f, i: (i, f)),
        out_shape=jax.ShapeDtypeStruct((s, D_FF), jnp.bfloat16),
        scratch_shapes=[pltpu.VMEM((d, tf), jnp.bfloat16),
                        pltpu.VMEM((d, tf), jnp.bfloat16)],
        compiler_params=_compiler_params(("arbitrary", "arbitrary")),
        name="ffn_in",
    )(h2, w_in, w_in)


def _ffn_out_kernel(act_ref, x1_ref, wout_ref, g_ref, *out_refs, final):
    for m in range(act_ref.shape[0] // FFN_OUT_SUB_M):
        rows = slice(m * FFN_OUT_SUB_M, (m + 1) * FFN_OUT_SUB_M)
        x2 = x1_ref[rows, :] + jnp.dot(act_ref[rows, :], wout_ref[...],
                                       preferred_element_type=jnp.float32)
        normed = _rms_norm_f32(x2, g_ref[...])
        if final:
            out_refs[0][rows, :] = normed
        else:
            out_refs[0][rows, :] = x2
            out_refs[1][rows, :] = normed.astype(out_refs[1].dtype)


def _ffn_out(act, x1, w_out, g_next, layer, final):
    s, d = x1.shape
    tm = FFN_OUT_TM
    rows = lambda width: pl.BlockSpec((tm, width), lambda i: (i, 0))
    out_specs = [rows(d)] if final else [rows(d), rows(d)]
    out_shape = [jax.ShapeDtypeStruct((s, d), jnp.float32)]
    if not final:
        out_shape.append(jax.ShapeDtypeStruct((s, d), jnp.bfloat16))
    return pl.pallas_call(
        functools.partial(_ffn_out_kernel, final=final),
        grid=(s // tm,),
        in_specs=[rows(D_FF), rows(d), _resident(w_out.shape, layer),
                  _resident(g_next.shape, layer)],
        out_specs=out_specs,
        out_shape=out_shape,
        compiler_params=_compiler_params(("arbitrary",)),
        name="ffn_out",
    )(act, x1, w_out, g_next)


def kernel(x, g_mix, w_in, w_conv, w_conv_out, w_ret_out, w_out, g_ffn, w_ffn_in,
           w_ffn_out, g_final):
    bsz, seq, d = x.shape
    depth = w_in.shape[0]
    assert bsz == 1 and d == D_MODEL and seq % INPROJ_TM == 0
    bf16 = jnp.bfloat16
    tables = _retention_tables(seq)
    w_co, w_ro, w_o = w_conv_out.astype(bf16), w_ret_out.astype(bf16), w_out.astype(bf16)
    w_fo = w_ffn_out.astype(bf16)
    w_cv = w_conv.reshape(depth, CONV_K, CONV_WIDTH)
    g_next = jnp.concatenate([g_mix[1:], g_final[None]], axis=0).reshape(depth, 1, d)
    g_ffn3 = g_ffn.reshape(depth, 1, d)
    xs = x.reshape(seq, d)
    h = _rms_norm(xs, g_mix[0], bf16)
    for l in range(depth):
        proj = _in_proj(h, w_in, l)
        x1, h2 = _mix_out(proj, xs, w_cv, tables, w_co, w_ro, w_o, g_ffn3, l)
        act = _ffn_in(h2, w_ffn_in, l)
        if l == depth - 1:
            (xs,) = _ffn_out(act, x1, w_fo, g_next, l, True)
        else:
            xs, h = _ffn_out(act, x1, w_fo, g_next, l, False)
    return xs.reshape(bsz, seq, d)
```
